```python
import math
import functools
import jax
import jax.numpy as jnp
from jax import lax
import numpy as np

D_MODEL = 2048
BATCH = 4
SEQ = 2048
DEPTH = 2
DEC_BATCH = 128
DEC_SEQ = 8
PAST_LEN = 8192
PAGE_SIZE = 128

F32 = jnp.float32
EPS = 1e-6
CONV_W = 4
LRU_WIDTH = D_MODEL
LRU_BLOCKS = 16
LRU_BLOCK = LRU_WIDTH // LRU_BLOCKS
LRU_C = 8.0
MLA_HEADS = 16
MLA_NOPE = 128
MLA_ROPE = 64
MLA_V = 128
MLA_Q_RANK = 512
MLA_KV_RANK = 512
MLA_SCALE = (MLA_NOPE + MLA_ROPE) ** -0.5
ROPE_THETA = 10000.0
Q_BLOCK = 128
SSD_INNER = D_MODEL
SSD_HEAD_DIM = 64
SSD_HEADS = SSD_INNER // SSD_HEAD_DIM
SSD_GROUPS = 8
SSD_STATE = 128
SSD_CHUNK = 128
SSD_CONV_DIM = SSD_INNER + 2 * SSD_GROUPS * SSD_STATE
N_BRANCH = 3
MEM_LEN = 256
XA_HEADS = 4
XA_HEAD_DIM = 128
XA_WIDTH = XA_HEADS * XA_HEAD_DIM
D_FF = 3 * D_MODEL
FFN_CONV_W = 3
IN_SPLITS = (LRU_WIDTH, MLA_Q_RANK, MLA_KV_RANK, MLA_ROPE, SSD_INNER, SSD_CONV_DIM, SSD_HEADS, N_BRANCH * D_MODEL)
D_IN = sum(IN_SPLITS)

kernel_name = 'hybrid_lru_mla_ssd_decoder_step'


def rms_norm(x, g):
    xf = x.astype(F32)
    y = xf * lax.rsqrt(jnp.mean(xf * xf, axis=-1, keepdims=True) + EPS)
    return (y * g.astype(F32)).astype(x.dtype)


def causal_dwconv(x, buf, w, b):
    width, L = w.shape[0], x.shape[1]
    xp = jnp.concatenate([buf.astype(x.dtype), x], axis=1)
    y = xp[:, 0:L] * w[0]
    for k in range(1, width):
        y = y + xp[:, k:k + L] * w[k]
    return y + b, xp[:, L:]


def rope_tables(pos):
    inv = 1.0 / (ROPE_THETA ** (jnp.arange(0, MLA_ROPE, 2, dtype=F32) / MLA_ROPE))
    ang = pos.astype(F32)[:, None] * inv[None, :]
    return jnp.cos(ang), jnp.sin(ang)


def apply_rope(x, cos, sin):
    x1, x2 = jnp.split(x.astype(F32), 2, axis=-1)
    return jnp.concatenate([x1 * cos - x2 * sin, x1 * sin + x2 * cos], axis=-1).astype(x.dtype)


def rg_lru(xc, h0, w_a, b_a, w_x, b_x, lam):
    bsz, L, W = xc.shape
    xf = xc.astype(F32)
    xb = xf.reshape(bsz, L, LRU_BLOCKS, LRU_BLOCK)
    r = jax.nn.sigmoid(jnp.einsum('blki,kij->blkj', xb, w_a.astype(F32)).reshape(bsz, L, W) + b_a.astype(F32))
    i = jax.nn.sigmoid(jnp.einsum('blki,kij->blkj', xb, w_x.astype(F32)).reshape(bsz, L, W) + b_x.astype(F32))
    log_a = -LRU_C * r * jax.nn.softplus(-lam.astype(F32))
    a = jnp.exp(log_a)
    u = jnp.sqrt(-jnp.expm1(2.0 * log_a)) * (i * xf)
    u = u.at[:, 0].add(a[:, 0] * h0.astype(F32))

    def combine(left, right):
        a1, b1 = left
        a2, b2 = right
        return a1 * a2, a2 * b1 + b2

    _, h = lax.associative_scan(combine, (a, u), axis=1)
    return h, h[:, -1]


def ssd_chunked(xs, dt, a, bm, cm, h0):
    b, l, nh, p = xs.shape
    g, n = bm.shape[2], bm.shape[3]
    e = nh // g
    q = SSD_CHUNK if l % SSD_CHUNK == 0 else l
    c = l // q
    xd = (xs * dt[..., None]).reshape(b, c, q, g, e, p)
    acum = jnp.cumsum((dt * a).reshape(b, c, q, g, e), axis=2)
    bm = bm.reshape(b, c, q, g, n)
    cm = cm.reshape(b, c, q, g, n)
    causal = jnp.tril(jnp.ones((q, q), dtype=bool))[:, :, None, None]
    seg = jnp.exp(jnp.where(causal, acum[:, :, :, None] - acum[:, :, None], -jnp.inf))
    cb = jnp.einsum('bclgn,bcsgn->bclsg', cm, bm)
    y_diag = jnp.einsum('bclsge,bcsgep->bclgep', cb[..., None] * seg, xd)
    decay_in = jnp.exp(acum[:, :, -1:] - acum)
    chunk_states = jnp.einsum('bcsgn,bcsgep->bcgepn', bm, xd * decay_in[..., None])
    chunk_decay = jnp.exp(acum[:, :, -1])

    def step(carry, inp):
        st, dec = inp
        return carry * dec[..., None, None] + st, carry

    h_last, h_prev = lax.scan(step, h0.reshape(b, g, e, p, n),
                              (jnp.moveaxis(chunk_states, 1, 0), jnp.moveaxis(chunk_decay, 1, 0)))
    h_prev = jnp.moveaxis(h_prev, 0, 1)
    y_off = jnp.einsum('bclgn,bcgepn->bclgep', cm, h_prev) * jnp.exp(acum)[..., None]
    return (y_diag + y_off).reshape(b, l, nh, p), h_last.reshape(b, nh, p, n)


def ssd_mixer(z, xbc, dt_raw, buf, h0, conv_w, conv_b, dt_bias, a_log, d_skip, norm_g):
    xbc, new_buf = causal_dwconv(xbc, buf, conv_w, conv_b)
    xbc = jax.nn.silu(xbc.astype(F32))
    xs, bm, cm = jnp.split(xbc, [SSD_INNER, SSD_INNER + SSD_GROUPS * SSD_STATE], axis=-1)
    bsz, L, _ = xs.shape
    xs = xs.reshape(bsz, L, SSD_HEADS, SSD_HEAD_DIM)
    bm = bm.reshape(bsz, L, SSD_GROUPS, SSD_STATE)
    cm = cm.reshape(bsz, L, SSD_GROUPS, SSD_STATE)
    dt = jax.nn.softplus(dt_raw.astype(F32) + dt_bias.astype(F32))
    a = -jnp.exp(a_log.astype(F32))
    y, h_last = ssd_chunked(xs, dt, a, bm, cm, h0.astype(F32))
    y = y + d_skip.astype(F32)[:, None] * xs
    y = y.reshape(bsz, L, SSD_INNER) * jax.nn.silu(z.astype(F32))
    return rms_norm(y, norm_g), h_last, new_buf


def mla_attend_prompt(q_lat, q_rope, ckv, kr):
    bsz, L = q_lat.shape[0], q_lat.shape[1]
    nb = L // Q_BLOCK
    qb = jnp.moveaxis(q_lat.reshape(bsz, nb, Q_BLOCK, MLA_HEADS, MLA_KV_RANK), 1, 0)
    rb = jnp.moveaxis(q_rope.reshape(bsz, nb, Q_BLOCK, MLA_HEADS, MLA_ROPE), 1, 0)
    kpos = jnp.arange(L)

    def block(args):
        ql, qr, start = args
        qpos = start + jnp.arange(Q_BLOCK)
        s = jnp.einsum('bqhr,bkr->bhqk', ql, ckv) + jnp.einsum('bqhd,bkd->bhqk', qr, kr)
        s = jnp.where(kpos[None, :] <= qpos[:, None], s.astype(F32) * MLA_SCALE, -jnp.inf)
        pr = jax.nn.softmax(s, axis=-1).astype(ckv.dtype)
        return jnp.einsum('bhqk,bkr->bqhr', pr, ckv)

    o = lax.map(block, (qb, rb, jnp.arange(nb) * Q_BLOCK))
    return jnp.moveaxis(o, 0, 1).reshape(bsz, L, MLA_HEADS, MLA_KV_RANK)


def mla_attend_sample(q_lat, q_rope, ckv, kr, pool_c, pool_k, layer, page_table):
    T = q_lat.shape[1]
    past = page_table.shape[1] * PAGE_SIZE
    kpos = jnp.arange(past + T)
    qpos = past + jnp.arange(T)
    mask = kpos[None, :] <= qpos[:, None]

    def one(args):
        ql, qr, c_new, k_new, pages = args
        c_all = jnp.concatenate([pool_c[layer, pages].reshape(past, MLA_KV_RANK), c_new.astype(pool_c.dtype)], axis=0)
        k_all = jnp.concatenate([pool_k[layer, pages].reshape(past, MLA_ROPE), k_new.astype(pool_k.dtype)], axis=0)
        s = jnp.einsum('qhr,kr->hqk', ql, c_all) + jnp.einsum('qhd,kd->hqk', qr, k_all)
        s = jnp.where(mask[None], s.astype(F32) * MLA_SCALE, -jnp.inf)
        pr = jax.nn.softmax(s, axis=-1).astype(c_all.dtype)
        return jnp.einsum('hqk,kr->qhr', pr, c_all)

    return lax.map(one, (q_lat, q_rope, ckv, kr, page_table))


def mem_kv(mem, norm_g, wk, wv):
    bsz = mem.shape[0]
    m = rms_norm(mem, norm_g)
    k = (m @ wk).reshape(bsz, MEM_LEN, XA_HEADS, XA_HEAD_DIM)
    v = (m @ wv).reshape(bsz, MEM_LEN, XA_HEADS, XA_HEAD_DIM)
    return k, v


def cross_attend(h, k, v, wq, wo):
    bsz, L, _ = h.shape
    q = (h @ wq).reshape(bsz, L, XA_HEADS, XA_HEAD_DIM)
    s = jnp.einsum('blhd,bmhd->bhlm', q, k.astype(q.dtype)).astype(F32) * (XA_HEAD_DIM ** -0.5)
    pr = jax.nn.softmax(s, axis=-1).astype(q.dtype)
    o = jnp.einsum('bhlm,bmhd->blhd', pr, v.astype(q.dtype)).reshape(bsz, L, XA_WIDTH)
    return o @ wo


def conv_ffn(h, buf, w_up, conv_w, conv_b, w_down):
    gate, up = jnp.split(h @ w_up, 2, axis=-1)
    gate, new_buf = causal_dwconv(gate, buf, conv_w, conv_b)
    return (jax.nn.gelu(gate) * up) @ w_down, new_buf


def trunk_layer(x, pos0, P, lru_h0, lru_buf0, ssd_h0, ssd_buf0, ffn_buf0, mem_k, mem_v, attend):
    bsz, L, _ = x.shape
    offs = np.cumsum(IN_SPLITS)[:-1].tolist()
    h = rms_norm(x, P['norm_mix'])
    x_lru, q_c, ckv_raw, kr_raw, z, xbc, dt_raw, gate_pre = jnp.split(h @ P['w_in'], offs, axis=-1)
    xc, lru_buf = causal_dwconv(x_lru, lru_buf0, P['lru_conv_w'], P['lru_conv_b'])
    y_a, lru_h = rg_lru(xc, lru_h0, P['lru_w_a'], P['lru_b_a'], P['lru_w_x'], P['lru_b_x'], P['lru_lambda'])
    cos, sin = rope_tables(pos0 + jnp.arange(L))
    q = (rms_norm(q_c, P['mla_q_norm']) @ P['mla_w_qb']).reshape(bsz, L, MLA_HEADS, MLA_NOPE + MLA_ROPE)
    q_nope = q[..., :MLA_NOPE]
    q_rope = apply_rope(q[..., MLA_NOPE:], cos[:, None], sin[:, None])
    ckv = rms_norm(ckv_raw, P['mla_kv_norm'])
    kr = apply_rope(kr_raw, cos, sin)
    q_lat = jnp.einsum('blhd,rhd->blhr', q_nope, P['mla_w_uk'])
    o_lat = attend(q_lat, q_rope, ckv, kr)
    y_b = jnp.einsum('blhr,rhv->blhv', o_lat, P['mla_w_uv']).reshape(bsz, L, MLA_HEADS * MLA_V)
    y_c, ssd_h, ssd_buf = ssd_mixer(z, xbc, dt_raw, ssd_buf0, ssd_h0, P['ssd_conv_w'], P['ssd_conv_b'],
                                    P['ssd_dt_bias'], P['ssd_a_log'], P['ssd_d'], P['ssd_norm'])
    g = jax.nn.sigmoid(gate_pre.astype(F32)).reshape(bsz, L, N_BRANCH, D_MODEL)
    merged = g[:, :, 0] * y_a + g[:, :, 1] * y_b.astype(F32) + g[:, :, 2] * y_c.astype(F32)
    x = x + merged.astype(x.dtype) @ P['w_out']
    x = x + cross_attend(rms_norm(x, P['norm_xa']), mem_k, mem_v, P['xa_wq'], P['xa_wo'])
    f, ffn_buf = conv_ffn(rms_norm(x, P['norm_ffn']), ffn_buf0, P['ffn_w_up'], P['ffn_conv_w'], P['ffn_conv_b'], P['ffn_w_down'])
    x = x + f
    return x, (ckv, kr, lru_h, lru_buf, ssd_h, ssd_buf, ffn_buf)


def setup_inputs(seed: int = 0) -> dict:
    key = jax.random.key(seed)
    ks = iter(jax.random.split(key, 64))

    def nrm(shape, scale=1.0):
        return jax.random.normal(next(ks), shape, F32) * scale

    def gain(shape):
        return 1.0 + 0.01 * nrm(shape)

    n_pages = PAST_LEN // PAGE_SIZE
    n_used = DEC_BATCH * n_pages
    n_pool = n_used + n_used // 4
    page_table = jax.random.permutation(next(ks), n_pool)[:n_used].reshape(DEC_BATCH, n_pages).astype(jnp.int32)
    u = jax.random.uniform(next(ks), (DEPTH, LRU_WIDTH), F32, 0.9, 0.999)
    s = u ** (1.0 / LRU_C)
    lru_lambda = jnp.log(s) - jnp.log1p(-s)
    dt0 = jnp.exp(jax.random.uniform(next(ks), (DEPTH, SSD_HEADS), F32, math.log(1e-3), math.log(1e-1)))
    ssd_dt_bias = dt0 + jnp.log(-jnp.expm1(-dt0))
    ssd_a_log = jnp.log(jax.random.uniform(next(ks), (DEPTH, SSD_HEADS), F32, 1.0, 16.0))
    return {
        'x_prompt': nrm((BATCH, SEQ, D_MODEL)),
        'x_sample': nrm((DEC_BATCH, DEC_SEQ, D_MODEL)),
        'cache_ckv': nrm((DEPTH, n_pool, PAGE_SIZE, MLA_KV_RANK)),
        'cache_krope': nrm((DEPTH, n_pool, PAGE_SIZE, MLA_ROPE)),
        'cache_mem_k': nrm((DEPTH, DEC_BATCH, MEM_LEN, XA_HEADS, XA_HEAD_DIM)),
        'cache_mem_v': nrm((DEPTH, DEC_BATCH, MEM_LEN, XA_HEADS, XA_HEAD_DIM)),
        'state_lru_h': nrm((DEPTH, DEC_BATCH, LRU_WIDTH), 0.5),
        'state_lru_conv': nrm((DEPTH, DEC_BATCH, CONV_W - 1, LRU_WIDTH)),
        'state_ssd_h': nrm((DEPTH, DEC_BATCH, SSD_HEADS, SSD_HEAD_DIM, SSD_STATE), 0.5),
        'state_ssd_conv': nrm((DEPTH, DEC_BATCH, CONV_W - 1, SSD_CONV_DIM)),
        'state_ffn_conv': nrm((DEPTH, DEC_BATCH, FFN_CONV_W - 1, D_FF)),
        'page_table': page_table,
        'mem_prompt': nrm((BATCH, MEM_LEN, D_MODEL)),
        'norm_mix': gain((DEPTH, D_MODEL)),
        'w_in': nrm((DEPTH, D_MODEL, D_IN), D_MODEL ** -0.5),
        'lru_conv_w': nrm((DEPTH, CONV_W, LRU_WIDTH), CONV_W ** -0.5),
        'lru_conv_b': nrm((DEPTH, LRU_WIDTH), 0.01),
        'lru_w_a': nrm((DEPTH, LRU_BLOCKS, LRU_BLOCK, LRU_BLOCK), LRU_BLOCK ** -0.5),
        'lru_b_a': nrm((DEPTH, LRU_WIDTH), 0.01),
        'lru_w_x': nrm((DEPTH, LRU_BLOCKS, LRU_BLOCK, LRU_BLOCK), LRU_BLOCK ** -0.5),
        'lru_b_x': nrm((DEPTH, LRU_WIDTH), 0.01),
        'lru_lambda': lru_lambda,
        'mla_q_norm': gain((DEPTH, MLA_Q_RANK)),
        'mla_w_qb': nrm((DEPTH, MLA_Q_RANK, MLA_HEADS * (MLA_NOPE + MLA_ROPE)), MLA_Q_RANK ** -0.5),
        'mla_kv_norm': gain((DEPTH, MLA_KV_RANK)),
        'mla_w_uk': nrm((DEPTH, MLA_KV_RANK, MLA_HEADS, MLA_NOPE), MLA_KV_RANK ** -0.5),
        'mla_w_uv': nrm((DEPTH, MLA_KV_RANK, MLA_HEADS, MLA_V), MLA_KV_RANK ** -0.5),
        'ssd_conv_w': nrm((DEPTH, CONV_W, SSD_CONV_DIM), CONV_W ** -0.5),
        'ssd_conv_b': nrm((DEPTH, SSD_CONV_DIM), 0.01),
        'ssd_dt_bias': ssd_dt_bias,
        'ssd_a_log': ssd_a_log,
        'ssd_d': gain((DEPTH, SSD_HEADS)),
        'ssd_norm': gain((DEPTH, SSD_INNER)),
        'w_out': nrm((DEPTH, D_MODEL, D_MODEL), D_MODEL ** -0.5),
        'norm_xa': gain((DEPTH, D_MODEL)),
        'norm_mem': gain((DEPTH, D_MODEL)),
        'xa_wq': nrm((DEPTH, D_MODEL, XA_WIDTH), D_MODEL ** -0.5),
        'xa_wk': nrm((DEPTH, D_MODEL, XA_WIDTH), D_MODEL ** -0.5),
        'xa_wv': nrm((DEPTH, D_MODEL, XA_WIDTH), D_MODEL ** -0.5),
        'xa_wo': nrm((DEPTH, XA_WIDTH, D_MODEL), XA_WIDTH ** -0.5),
        'norm_ffn': gain((DEPTH, D_MODEL)),
        'ffn_w_up': nrm((DEPTH, D_MODEL, 2 * D_FF), D_MODEL ** -0.5),
        'ffn_conv_w': nrm((DEPTH, FFN_CONV_W, D_FF), FFN_CONV_W ** -0.5),
        'ffn_conv_b': nrm((DEPTH, D_FF), 0.01),
        'ffn_w_down': nrm((DEPTH, D_FF, D_MODEL), D_FF ** -0.5),
        'norm_final': gain((D_MODEL,)),
    }


def reference(x_prompt, x_sample, cache_ckv, cache_krope, cache_mem_k, cache_mem_v, state_lru_h, state_lru_conv,
              state_ssd_h, state_ssd_conv, state_ffn_conv, page_table, mem_prompt,
              norm_mix, w_in, lru_conv_w, lru_conv_b, lru_w_a, lru_b_a, lru_w_x, lru_b_x, lru_lambda,
              mla_q_norm, mla_w_qb, mla_kv_norm, mla_w_uk, mla_w_uv,
              ssd_conv_w, ssd_conv_b, ssd_dt_bias, ssd_a_log, ssd_d, ssd_norm, w_out,
              norm_xa, norm_mem, xa_wq, xa_wk, xa_wv, xa_wo,
              norm_ffn, ffn_w_up, ffn_conv_w, ffn_conv_b, ffn_w_down, norm_final):
    bp, dtp = x_prompt.shape[0], x_prompt.dtype
    xp, xs = x_prompt, x_sample
    p_rows, s_rows, p_mem_ks, p_mem_vs = [], [], [], []
    for l in range(DEPTH):
        P = {
            'norm_mix': norm_mix[l], 'w_in': w_in[l],
            'lru_conv_w': lru_conv_w[l], 'lru_conv_b': lru_conv_b[l], 'lru_w_a': lru_w_a[l], 'lru_b_a': lru_b_a[l],
            'lru_w_x': lru_w_x[l], 'lru_b_x': lru_b_x[l], 'lru_lambda': lru_lambda[l],
            'mla_q_norm': mla_q_norm[l], 'mla_w_qb': mla_w_qb[l], 'mla_kv_norm': mla_kv_norm[l],
            'mla_w_uk': mla_w_uk[l], 'mla_w_uv': mla_w_uv[l],
            'ssd_conv_w': ssd_conv_w[l], 'ssd_conv_b': ssd_conv_b[l], 'ssd_dt_bias': ssd_dt_bias[l],
            'ssd_a_log': ssd_a_log[l], 'ssd_d': ssd_d[l], 'ssd_norm': ssd_norm[l], 'w_out': w_out[l],
            'norm_xa': norm_xa[l], 'xa_wq': xa_wq[l], 'xa_wo': xa_wo[l],
            'norm_ffn': norm_ffn[l], 'ffn_w_up': ffn_w_up[l], 'ffn_conv_w': ffn_conv_w[l],
            'ffn_conv_b': ffn_conv_b[l], 'ffn_w_down': ffn_w_down[l],
        }
        mk, mv = mem_kv(mem_prompt, norm_mem[l], xa_wk[l], xa_wv[l])
        p_mem_ks.append(mk)
        p_mem_vs.append(mv)
        xp, rows = trunk_layer(
            xp, 0, P,
            jnp.zeros((bp, LRU_WIDTH), dtp), jnp.zeros((bp, CONV_W - 1, LRU_WIDTH), dtp),
            jnp.zeros((bp, SSD_HEADS, SSD_HEAD_DIM, SSD_STATE), dtp), jnp.zeros((bp, CONV_W - 1, SSD_CONV_DIM), dtp),
            jnp.zeros((bp, FFN_CONV_W - 1, D_FF), dtp), mk, mv, mla_attend_prompt)
        p_rows.append(rows)
        attend_s = functools.partial(mla_attend_sample, pool_c=cache_ckv, pool_k=cache_krope, layer=l, page_table=page_table)
        xs, rows = trunk_layer(
            xs, PAST_LEN, P, state_lru_h[l], state_lru_conv[l], state_ssd_h[l], state_ssd_conv[l],
            state_ffn_conv[l], cache_mem_k[l], cache_mem_v[l], attend_s)
        s_rows.append(rows)
    y_prompt = rms_norm(xp, norm_final)
    y_sample = rms_norm(xs, norm_final)
    p_ckv, p_krope, p_lru_h, p_lru_conv, p_ssd_h, p_ssd_conv, p_ffn_conv = [jnp.stack(r) for r in zip(*p_rows)]
    s_ckv, s_krope, s_lru_h, s_lru_conv, s_ssd_h, s_ssd_conv, s_ffn_conv = [jnp.stack(r) for r in zip(*s_rows)]
    p_mem_k = jnp.stack(p_mem_ks)
    p_mem_v = jnp.stack(p_mem_vs)
    return (y_prompt, y_sample,
            p_ckv, p_krope, p_lru_h, p_lru_conv, p_ssd_h, p_ssd_conv, p_ffn_conv, p_mem_k, p_mem_v,
            s_ckv, s_krope, s_lru_h, s_lru_conv, s_ssd_h, s_ssd_conv, s_ffn_conv)
```

```python
import functools

import jax
import jax.numpy as jnp
from jax import lax
from jax.experimental import pallas as pl
from jax.experimental.pallas import tpu as pltpu

F32 = jnp.float32
BF16 = jnp.bfloat16
EPS = 1e-6
LRU_C = 8.0
ROPE_THETA = 10000.0
LANES = 128
SUBLANES = 8
VMEM_LIMIT_BYTES = 56 * 2**20
SSD_CHUNK = 128
SSD_STATE = 128
SSD_HEAD_DIM = 64
LRU_BLOCK = 128
PAGES_PER_STEP = 8
PROJ_BLOCK = 512


def _cparams(*sem):
    return pltpu.CompilerParams(dimension_semantics=sem, vmem_limit_bytes=VMEM_LIMIT_BYTES)


def _tile(n, pref, mult=SUBLANES):
    if n <= pref:
        return n
    t = pref - pref % mult
    while t > mult and n % t:
        t -= mult
    assert n % t == 0, (n, pref, mult)
    return t


def _rows(shape):
    return lax.broadcasted_iota(jnp.int32, shape, 0)


def _cols(shape):
    return lax.broadcasted_iota(jnp.int32, shape, 1)


def _rms(x, g):
    return x * lax.rsqrt(jnp.mean(x * x, axis=-1, keepdims=True) + EPS) * g


def _mm_kernel(*refs, has_gain, has_res, stage):
    it = iter(refs)
    a_ref = next(it)
    g_ref = next(it) if has_gain else None
    w_ref = next(it)
    r_ref = next(it) if has_res else None
    o_ref = next(it)
    h_ref = next(it) if stage else None
    if stage:
        @pl.when(pl.program_id(1) == 0)
        def _():
            a = a_ref[...].astype(F32)
            if has_gain:
                a = _rms(a, g_ref[...])
            h_ref[...] = a.astype(BF16)
        h = h_ref[...]
    else:
        h = a_ref[...]
    acc = jnp.dot(h, w_ref[...], preferred_element_type=F32)
    if has_res:
        acc = acc + r_ref[...]
    o_ref[...] = acc.astype(o_ref.dtype)


def mm(a, w, *, gain=None, res=None, a_col=0, out_dtype=F32, tm=1024, tn=512, name="mm"):
    M = a.shape[0]
    K, N = w.shape
    tm = _tile(M, tm)
    tn = _tile(N, tn, LANES)
    has_gain, has_res = gain is not None, res is not None
    stage = has_gain or a.dtype != BF16
    in_specs = [pl.BlockSpec((tm, K), lambda i, j: (i, a_col))]
    args = [a]
    if has_gain:
        in_specs.append(pl.BlockSpec((1, K), lambda i, j: (0, 0)))
        args.append(gain.reshape(1, K).astype(F32))
    in_specs.append(pl.BlockSpec((K, tn), lambda i, j: (0, j)))
    args.append(w)
    if has_res:
        in_specs.append(pl.BlockSpec((tm, tn), lambda i, j: (i, j)))
        args.append(res)
    return pl.pallas_call(
        functools.partial(_mm_kernel, has_gain=has_gain, has_res=has_res, stage=stage),
        grid=(M // tm, N // tn),
        in_specs=in_specs,
        out_specs=pl.BlockSpec((tm, tn), lambda i, j: (i, j)),
        out_shape=jax.ShapeDtypeStruct((M, N), out_dtype),
        scratch_shapes=[pltpu.VMEM((tm, K), BF16)] if stage else [],
        compiler_params=_cparams("parallel", "arbitrary"),
        name=name,
    )(*args)


def _hmm_kernel(a_ref, w_ref, o_ref):
    o_ref[...] = jnp.dot(a_ref[...].astype(BF16), w_ref[...], preferred_element_type=F32).astype(o_ref.dtype)


def hmm(a, w, *, a_stride=1, out_dtype=F32, tm=1024, name="hmm"):
    M = a.shape[0]
    H, Ka, Nb = w.shape
    tm = _tile(M, tm)
    return pl.pallas_call(
        _hmm_kernel,
        grid=(M // tm, H),
        in_specs=[pl.BlockSpec((tm, Ka), lambda i, h: (i, h * a_stride)),
                  pl.BlockSpec((None, Ka, Nb), lambda i, h: (h, 0, 0))],
        out_specs=pl.BlockSpec((tm, Nb), lambda i, h: (i, h)),
        out_shape=jax.ShapeDtypeStruct((M, H * Nb), out_dtype),
        compiler_params=_cparams("parallel", "parallel"),
        name=name,
    )(a, w)


def _shift_rows(x, s, width, prev, seg, halo):
    R = x.shape[0]
    xr = pltpu.roll(x, s, 0)
    if not halo:
        up = width - 1 - s
        pr = pltpu.roll(prev, (R - up) % R, 0) if up else prev
        return jnp.where(_rows(x.shape) % seg >= s, xr, pr)
    pr = pltpu.roll(prev, s, 0)
    top = jnp.where(_rows(pr.shape) >= s, xr[:SUBLANES], pr)
    if R == SUBLANES:
        return top
    return jnp.concatenate([top, xr[SUBLANES:]], axis=0)


def _causal_conv(x, prev, w, b, seg, halo):
    width = w.shape[0]
    y = x * w[width - 1:width]
    for s in range(1, width):
        y = y + _shift_rows(x, s, width, prev, seg, halo) * w[width - 1 - s:width - s]
    return y + b


def _pad_state_rows(buf):
    nb, wm1, c = buf.shape
    return jnp.pad(buf, ((0, 0), (0, SUBLANES - wm1), (0, 0))).reshape(nb * SUBLANES, c)


def _segment_scan(a, u, seg):
    rid = _rows(a.shape) % seg
    d = 1
    while d < seg:
        ok = rid >= d
        a_sh = pltpu.roll(a, d, 0)
        u_sh = pltpu.roll(u, d, 0)
        u = jnp.where(ok, a * u_sh + u, u)
        a = jnp.where(ok, a * a_sh, a)
        d *= 2
    return a, u


def _segment_cumsum(x, seg):
    rid = _rows(x.shape) % seg
    d = 1
    while d < seg:
        x = jnp.where(rid >= d, x + pltpu.roll(x, d, 0), x)
        d *= 2
    return x


def _lru_kernel(*refs, seg, carried):
    if carried:
        x_ref, prev_ref, cw_ref, cb_ref, wa_ref, wx_ref, ba_ref, bx_ref, lam_ref, y_ref, carry_ref = refs
        h0_ref = None
        first = pl.program_id(2) == 0

        @pl.when(first)
        def _():
            carry_ref[...] = jnp.zeros_like(carry_ref)
    else:
        x_ref, prev_ref, cw_ref, cb_ref, wa_ref, wx_ref, ba_ref, bx_ref, lam_ref, h0_ref, y_ref = refs
    for k in range(x_ref.shape[1] // LRU_BLOCK):
        sl = slice(k * LRU_BLOCK, (k + 1) * LRU_BLOCK)
        x = x_ref[:, sl]
        prev = prev_ref[:, sl]
        if carried:
            prev = jnp.where(first, 0.0, prev)
        xc = _causal_conv(x, prev, cw_ref[:, sl], cb_ref[:, sl], seg, carried)
        xb = xc.astype(BF16)
        r = jax.nn.sigmoid(jnp.dot(xb, wa_ref[k], preferred_element_type=F32) + ba_ref[:, sl])
        i = jax.nn.sigmoid(jnp.dot(xb, wx_ref[k], preferred_element_type=F32) + bx_ref[:, sl])
        lam = lam_ref[:, sl]
        softplus_neg_lam = jnp.maximum(-lam, 0.0) + jnp.log1p(jnp.exp(-jnp.abs(lam)))
        log_a = -LRU_C * r * softplus_neg_lam
        a = jnp.exp(log_a)
        u = jnp.sqrt(-jnp.tanh(log_a) * (a * a + 1.0)) * (i * xc)
        if not carried:
            u = u + a * h0_ref[:, sl]
        a_cum, h = _segment_scan(a, u, seg)
        if carried:
            h = h + a_cum * carry_ref[:, sl]
            carry_ref[:, sl] = h[seg - 1:seg]
        y_ref[:, sl] = h


def lru_mixer(proj, nseq, L, state_h, state_conv, wts, *, cb=512, rows=256):
    M = proj.shape[0]
    W = wts["lru_conv_w"].shape[1]
    cb = _tile(W, cb, LANES)
    nb = cb // LRU_BLOCK
    common = [wts["lru_conv_w"], wts["lru_conv_b"], wts["lru_w_a"], wts["lru_w_x"], wts["lru_b_a"],
              wts["lru_b_x"], wts["lru_lambda"]]
    if state_h is None:
        R = _tile(L, rows)
        nch = L // R

        def vec(n):
            return pl.BlockSpec((n, cb), lambda b, j, c: (0, j))

        def gates():
            return pl.BlockSpec((nb, LRU_BLOCK, LRU_BLOCK), lambda b, j, c: (j, 0, 0))

        in_specs = [
            pl.BlockSpec((R, cb), lambda b, j, c: (b * nch + c, j)),
            pl.BlockSpec((SUBLANES, cb), lambda b, j, c: (jnp.maximum((b * nch + c) * (R // SUBLANES) - 1, 0), j)),
            vec(4), vec(1), gates(), gates(), vec(1), vec(1), vec(1),
        ]
        return pl.pallas_call(
            functools.partial(_lru_kernel, seg=R, carried=True),
            grid=(nseq, W // cb, nch),
            in_specs=in_specs,
            out_specs=pl.BlockSpec((R, cb), lambda b, j, c: (b * nch + c, j)),
            out_shape=jax.ShapeDtypeStruct((M, W), F32),
            scratch_shapes=[pltpu.VMEM((1, cb), F32)],
            compiler_params=_cparams("parallel", "parallel", "arbitrary"),
            name="lru_prompt",
        )(proj, proj, *common)
    R = _tile(M, rows)
    h0_rows = jnp.pad(state_h[:, None, :], ((0, 0), (0, L - 1), (0, 0))).reshape(M, W)
    prev = _pad_state_rows(state_conv)

    def vec(n):
        return pl.BlockSpec((n, cb), lambda i, j: (0, j))

    def tile():
        return pl.BlockSpec((R, cb), lambda i, j: (i, j))

    def gates():
        return pl.BlockSpec((nb, LRU_BLOCK, LRU_BLOCK), lambda i, j: (j, 0, 0))

    in_specs = [tile(), tile(), vec(4), vec(1), gates(), gates(), vec(1), vec(1), vec(1), tile()]
    return pl.pallas_call(
        functools.partial(_lru_kernel, seg=L, carried=False),
        grid=(M // R, W // cb),
        in_specs=in_specs,
        out_specs=tile(),
        out_shape=jax.ShapeDtypeStruct((M, W), F32),
        compiler_params=_cparams("parallel", "parallel"),
        name="lru_sample",
    )(proj, prev, *common, h0_rows)


def _mla_prep_kernel(qa_ref, qb_ref, ckv_ref, kr_ref, cq_ref, sq_ref, ck_ref, sk_ref, g_ref,
                     q_ref, ckvn_ref, krr_ref):
    hw = cq_ref.shape[1]
    cq, sq = cq_ref[...], sq_ref[...]
    for h in range(q_ref.shape[1] // hw):
        sl = slice(h * hw, (h + 1) * hw)
        q_ref[:, sl] = (qa_ref[:, sl] * cq + qb_ref[:, sl] * sq).astype(q_ref.dtype)
    ckvn_ref[...] = _rms(ckv_ref[...], g_ref[...])
    kr = kr_ref[...]
    krr_ref[...] = kr[:, :LANES] * ck_ref[...] + kr[:, LANES:2 * LANES] * sk_ref[...]


def mla_prep(q2, proj, tables, kv_gain, cols, *, tm=256):
    M = q2.shape[0]
    qw = q2.shape[1] // 2
    cq, sq, ck, sk = tables
    tm = _tile(min(M, cq.shape[0]), tm)
    nt = cq.shape[0] // tm
    R = kv_gain.shape[0]

    def tab(w):
        return pl.BlockSpec((tm, w), lambda i: (i % nt, 0))

    return pl.pallas_call(
        _mla_prep_kernel,
        grid=(M // tm,),
        in_specs=[pl.BlockSpec((tm, qw), lambda i: (i, 0)), pl.BlockSpec((tm, qw), lambda i: (i, 1)),
                  pl.BlockSpec((tm, R), lambda i: (i, cols["ckv"] // R)),
                  pl.BlockSpec((tm, PROJ_BLOCK), lambda i: (i, cols["small"] // PROJ_BLOCK)),
                  tab(cq.shape[1]), tab(cq.shape[1]), tab(LANES), tab(LANES),
                  pl.BlockSpec((1, R), lambda i: (0, 0))],
        out_specs=[pl.BlockSpec((tm, qw), lambda i: (i, 0)), pl.BlockSpec((tm, R), lambda i: (i, 0)),
                   pl.BlockSpec((tm, LANES), lambda i: (i, 0))],
        out_shape=[jax.ShapeDtypeStruct((M, qw), BF16), jax.ShapeDtypeStruct((M, R), F32),
                   jax.ShapeDtypeStruct((M, LANES), F32)],
        compiler_params=_cparams("parallel"),
        name="mla_prep",
    )(q2, q2, proj, proj, cq, sq, ck, sk, kv_gain.reshape(1, R))


def _mha_kernel(q_ref, kn_ref, kr_ref, v_ref, o_ref, *, tk, scale):
    qi = pl.program_id(2)
    q = q_ref[...]
    tq = q.shape[0]

    def scores(kb):
        rows = pl.ds(pl.multiple_of(kb * tk, tk), tk)
        k = jnp.concatenate([kn_ref[rows, :], kr_ref[rows, :].astype(BF16)], axis=1)
        s = lax.dot_general(q, k, (((1,), (1,)), ((), ())), preferred_element_type=F32) * scale
        return s, v_ref[rows, :]

    def update(carry, s, v):
        m, l, acc = carry
        m_new = jnp.maximum(m, jnp.max(s, axis=-1, keepdims=True))
        alpha = jnp.exp(m - m_new)
        p = jnp.exp(s - m_new)
        l = alpha * l + jnp.sum(p, axis=-1, keepdims=True)
        acc = alpha * acc + jnp.dot(p.astype(BF16), v, preferred_element_type=F32)
        return m_new, l, acc

    def body(kb, carry):
        s, v = scores(kb)
        return update(carry, s, v)

    init = (jnp.full((tq, 1), -jnp.inf, F32), jnp.zeros((tq, 1), F32), jnp.zeros((tq, v_ref.shape[1]), F32))
    carry = lax.fori_loop(0, qi, body, init)
    s, v = scores(qi)
    s = jnp.where(_cols(s.shape) <= _rows(s.shape), s, -jnp.inf)
    m, l, acc = update(carry, s, v)
    o_ref[...] = (acc / l).astype(o_ref.dtype)


def mha_prompt(q, kn, kr, v, nseq, L, heads, scale, *, tq=256):
    M = q.shape[0]
    hw = q.shape[1] // heads
    dv = v.shape[1] // heads
    tq = _tile(L, tq)
    nq = L // tq
    return pl.pallas_call(
        functools.partial(_mha_kernel, tk=tq, scale=scale),
        grid=(nseq, heads, nq),
        in_specs=[pl.BlockSpec((tq, hw), lambda b, h, i: (b * nq + i, h)),
                  pl.BlockSpec((L, hw - LANES), lambda b, h, i: (b, h)),
                  pl.BlockSpec((L, LANES), lambda b, h, i: (b, 0)),
                  pl.BlockSpec((L, dv), lambda b, h, i: (b, h))],
        out_specs=pl.BlockSpec((tq, dv), lambda b, h, i: (b * nq + i, h)),
        out_shape=jax.ShapeDtypeStruct((M, heads * dv), F32),
        compiler_params=_cparams("parallel", "parallel", "arbitrary"),
        name="mha_prompt",
    )(q, kn, kr, v)


def _mqa_paged_kernel(pt_ref, ql_ref, qr_ref, *refs, pps, rope, heads, scale):
    del pt_ref
    c_refs = refs[:pps]
    k_refs = refs[pps:2 * pps]
    cn_ref, kn_ref, o_ref, m_ref, l_ref, acc_ref = refs[2 * pps:]
    step = pl.program_id(1)

    @pl.when(step == 0)
    def _():
        m_ref[...] = jnp.full_like(m_ref, -jnp.inf)
        l_ref[...] = jnp.zeros_like(l_ref)
        acc_ref[...] = jnp.zeros_like(acc_ref)

    ql = ql_ref[...]
    qr = qr_ref[:, LANES:LANES + rope]
    dn = (((1,), (1,)), ((), ()))

    def update(s_list, c_list):
        m = m_ref[...]
        m_new = m
        for s in s_list:
            m_new = jnp.maximum(m_new, jnp.max(s, axis=-1, keepdims=True))
        alpha = jnp.exp(m - m_new)
        l = alpha * l_ref[...]
        acc = alpha * acc_ref[...]
        for s, c in zip(s_list, c_list):
            p = jnp.exp(s - m_new)
            l = l + jnp.sum(p, axis=-1, keepdims=True)
            acc = acc + jnp.dot(p.astype(BF16), c, preferred_element_type=F32)
        m_ref[...] = m_new
        l_ref[...] = l
        acc_ref[...] = acc

    s_list, c_list = [], []
    for c_ref, k_ref in zip(c_refs, k_refs):
        c = c_ref[...].astype(BF16)
        s = lax.dot_general(ql, c, dn, preferred_element_type=F32)
        s = s + lax.dot_general(qr, k_ref[...].astype(BF16), dn, preferred_element_type=F32)
        s_list.append(s * scale)
        c_list.append(c)
    update(s_list, c_list)

    @pl.when(step == pl.num_programs(1) - 1)
    def _():
        c = cn_ref[...].astype(BF16)
        s = lax.dot_general(ql, c, dn, preferred_element_type=F32)
        s = s + lax.dot_general(qr, kn_ref[:, :rope].astype(BF16), dn, preferred_element_type=F32)
        s = jnp.where(_cols(s.shape) <= _rows(s.shape) // heads, s * scale, -jnp.inf)
        update([s], [c])
        o_ref[...] = (acc_ref[...] / l_ref[...]).astype(o_ref.dtype)


def mqa_paged(q_lat, q_all, ckv_new, kr_new, pool_c, pool_k, layer, page_table, heads, scale):
    nb, n_pages = page_table.shape
    T = ckv_new.shape[0] // nb
    R = pool_c.shape[-1]
    rope = pool_k.shape[-1]
    page = pool_c.shape[2]
    pps = _tile(n_pages, PAGES_PER_STEP, 1)
    rows = T * heads
    ql = q_lat.reshape(nb * rows, R)
    qa = q_all.reshape(nb * rows, q_all.shape[1] // heads)

    def page_spec(width, k):
        return pl.BlockSpec((None, None, page, width), lambda b, s, pt: (layer, pt[b, s * pps + k], 0, 0))

    grid_spec = pltpu.PrefetchScalarGridSpec(
        num_scalar_prefetch=1,
        grid=(nb, n_pages // pps),
        in_specs=[pl.BlockSpec((rows, R), lambda b, s, pt: (b, 0)),
                  pl.BlockSpec((rows, qa.shape[1]), lambda b, s, pt: (b, 0))]
                 + [page_spec(R, k) for k in range(pps)] + [page_spec(rope, k) for k in range(pps)]
                 + [pl.BlockSpec((T, R), lambda b, s, pt: (b, 0)),
                    pl.BlockSpec((T, LANES), lambda b, s, pt: (b, 0))],
        out_specs=pl.BlockSpec((rows, R), lambda b, s, pt: (b, 0)),
        scratch_shapes=[pltpu.VMEM((rows, 1), F32), pltpu.VMEM((rows, 1), F32), pltpu.VMEM((rows, R), F32)],
    )
    out = pl.pallas_call(
        functools.partial(_mqa_paged_kernel, pps=pps, rope=rope, heads=heads, scale=scale),
        grid_spec=grid_spec,
        out_shape=jax.ShapeDtypeStruct((nb * rows, R), BF16),
        compiler_params=_cparams("parallel", "arbitrary"),
        name="mqa_paged",
    )(page_table, ql, qa, *([pool_c] * pps), *([pool_k] * pps), ckv_new, kr_new)
    return out.reshape(nb * T, heads * R)


def _ssd_kernel(*refs, seg, carried, heads, conv_cb):
    if carried:
        (z_ref, xbc_ref, prev_ref, dt_ref, cw_ref, cb_ref, dtb_ref, alog_ref, dsk_ref, ng_ref,
         y_ref, hout_ref, xc_ref, yacc_ref, acum_ref, dts_ref, st_ref) = refs
        h0_ref = None
        first = pl.program_id(1) == 0

        @pl.when(first)
        def _():
            st_ref[...] = jnp.zeros_like(st_ref)
    else:
        (z_ref, xbc_ref, prev_ref, dt_ref, cw_ref, cb_ref, dtb_ref, alog_ref, dsk_ref, ng_ref, h0_ref,
         y_ref, hout_ref, xc_ref, yacc_ref, acum_ref, dts_ref) = refs
    R = z_ref.shape[0]
    inner = z_ref.shape[1]
    P = inner // heads
    N = SSD_STATE
    G = (xbc_ref.shape[1] - inner) // (2 * N)
    hpg = heads // G
    dn_t = (((1,), (1,)), ((), ()))

    for k in range(xbc_ref.shape[1] // conv_cb):
        sl = slice(k * conv_cb, (k + 1) * conv_cb)
        prev = prev_ref[:, sl]
        if carried:
            prev = jnp.where(first, 0.0, prev)
        xc_ref[:, sl] = jax.nn.silu(
            _causal_conv(xbc_ref[:, sl], prev, cw_ref[:, sl], cb_ref[:, sl], seg, carried))

    dt = jax.nn.softplus(dt_ref[...] + dtb_ref[...])
    acum = _segment_cumsum(dt * -jnp.exp(alog_ref[...]), seg)
    dts_ref[...] = dt
    acum_ref[...] = acum
    if R < LANES:
        acum_sq = jnp.concatenate([acum, jnp.zeros((LANES - R, LANES), F32)], axis=0)
    else:
        acum_sq = acum
    acum_t = acum_sq.T
    rid, cid = _rows((R, R)), _cols((R, R))
    allowed = (cid <= rid) & (cid // seg == rid // seg)
    dsk = dsk_ref[...]

    for g in range(G):
        bm = xc_ref[:, inner + g * N: inner + (g + 1) * N].astype(BF16)
        cm = xc_ref[:, inner + (G + g) * N: inner + (G + g + 1) * N].astype(BF16)
        cb_mat = lax.dot_general(cm, bm, dn_t, preferred_element_type=F32)
        for e in range(hpg):
            h = g * hpg + e
            xs = xc_ref[:, h * P:(h + 1) * P]
            decay = jnp.exp(jnp.where(allowed, acum[:, h:h + 1] - acum_t[h:h + 1, :R], -jnp.inf))
            xd = xs * dt[:, h:h + 1]
            y = jnp.dot((cb_mat * decay).astype(BF16), xd.astype(BF16), preferred_element_type=F32)
            yacc_ref[:, h * P:(h + 1) * P] = y + dsk[:, h:h + 1] * xs

    def per_segment(s, carry):
        rows = pl.ds(pl.multiple_of(s * seg, seg), seg)
        ac = acum_ref[rows, :]
        dtr = dts_ref[rows, :]
        for g in range(G):
            bm = xc_ref[rows, inner + g * N: inner + (g + 1) * N].astype(BF16)
            cm = xc_ref[rows, inner + (G + g) * N: inner + (G + g + 1) * N].astype(BF16)
            for e in range(hpg):
                h = g * hpg + e
                acol = ac[:, h:h + 1]
                hprev = st_ref[h] if carried else h0_ref[s, h]
                xs = xc_ref[rows, h * P:(h + 1) * P]
                yoff = lax.dot_general(cm, hprev.astype(BF16), dn_t, preferred_element_type=F32)
                yacc_ref[rows, h * P:(h + 1) * P] += yoff * jnp.exp(acol)
                alast = acol[seg - 1:seg, :]
                xdw = (xs * dtr[:, h:h + 1] * jnp.exp(alast - acol)).astype(BF16)
                st = lax.dot_general(xdw, bm, (((0,), (0,)), ((), ())), preferred_element_type=F32)
                hnew = hprev * jnp.exp(alast) + st
                if carried:
                    st_ref[h] = hnew
                    hout_ref[h] = hnew
                else:
                    hout_ref[s, h] = hnew
        return carry

    if R == seg:
        per_segment(0, 0)
    else:
        lax.fori_loop(0, R // seg, per_segment, 0)

    y_ref[...] = _rms(yacc_ref[...] * jax.nn.silu(z_ref[...]), ng_ref[...])


def ssd_mixer(proj, nseq, L, state_h, state_conv, wts, cols, *, seqs_per_tile=8):
    M = proj.shape[0]
    heads = wts["ssd_heads"]
    inner = heads * SSD_HEAD_DIM
    cdim = wts["ssd_conv_w"].shape[1]
    conv_cb = PROJ_BLOCK
    zc, xc, dc = cols["z"] // inner, cols["xbc"] // cdim, (cols["small"] + 2 * LANES) // LANES
    assert cols["z"] % inner == 0 and cols["xbc"] % cdim == 0
    vecs = [wts["ssd_conv_w"], wts["ssd_conv_b"], wts["ssd_dt_bias"], wts["ssd_a_log"], wts["ssd_d"],
            wts["ssd_norm"]]
    widths = [(4, cdim), (1, cdim), (1, LANES), (1, LANES), (1, LANES), (1, inner)]
    st_shape = (heads, SSD_HEAD_DIM, SSD_STATE)
    if state_h is None:
        Q = SSD_CHUNK if L % SSD_CHUNK == 0 else L
        nch = L // Q
        in_specs = [pl.BlockSpec((Q, inner), lambda b, c: (b * nch + c, zc)),
                    pl.BlockSpec((Q, cdim), lambda b, c: (b * nch + c, xc)),
                    pl.BlockSpec((SUBLANES, cdim),
                                 lambda b, c: (jnp.maximum((b * nch + c) * (Q // SUBLANES) - 1, 0), xc)),
                    pl.BlockSpec((Q, LANES), lambda b, c: (b * nch + c, dc))]
        in_specs += [pl.BlockSpec(w, lambda b, c: (0, 0)) for w in widths]
        return pl.pallas_call(
            functools.partial(_ssd_kernel, seg=Q, carried=True, heads=heads, conv_cb=conv_cb),
            grid=(nseq, nch),
            in_specs=in_specs,
            out_specs=[pl.BlockSpec((Q, inner), lambda b, c: (b * nch + c, 0)),
                       pl.BlockSpec((None,) + st_shape, lambda b, c: (b, 0, 0, 0))],
            out_shape=[jax.ShapeDtypeStruct((M, inner), F32), jax.ShapeDtypeStruct((nseq,) + st_shape, F32)],
            scratch_shapes=[pltpu.VMEM((Q, cdim), F32), pltpu.VMEM((Q, inner), F32),
                            pltpu.VMEM((Q, LANES), F32), pltpu.VMEM((Q, LANES), F32),
                            pltpu.VMEM(st_shape, F32)],
            compiler_params=_cparams("parallel", "arbitrary"),
            name="ssd_prompt",
        )(proj, proj, proj, proj, *vecs)
    ns = _tile(nseq, seqs_per_tile, 1)
    R = ns * L
    prev = _pad_state_rows(state_conv)
    in_specs = [pl.BlockSpec((R, inner), lambda i: (i, zc)),
                pl.BlockSpec((R, cdim), lambda i: (i, xc)),
                pl.BlockSpec((R, cdim), lambda i: (i, 0)),
                pl.BlockSpec((R, LANES), lambda i: (i, dc))]
    in_specs += [pl.BlockSpec(w, lambda i: (0, 0)) for w in widths]
    in_specs += [pl.BlockSpec((ns,) + st_shape, lambda i: (i, 0, 0, 0))]
    return pl.pallas_call(
        functools.partial(_ssd_kernel, seg=L, carried=False, heads=heads, conv_cb=conv_cb),
        grid=(M // R,),
        in_specs=in_specs,
        out_specs=[pl.BlockSpec((R, inner), lambda i: (i, 0)),
                   pl.BlockSpec((ns,) + st_shape, lambda i: (i, 0, 0, 0))],
        out_shape=[jax.ShapeDtypeStruct((M, inner), F32), jax.ShapeDtypeStruct((nseq,) + st_shape, F32)],
        scratch_shapes=[pltpu.VMEM((R, cdim), F32), pltpu.VMEM((R, inner), F32),
                        pltpu.VMEM((R, LANES), F32), pltpu.VMEM((R, LANES), F32)],
        compiler_params=_cparams("parallel"),
        name="ssd_sample",
    )(proj, proj, prev, proj, *vecs, state_h)


def _merge_kernel(g0_ref, g1_ref, g2_ref, ya_ref, yb_ref, yc_ref, w_ref, x_ref, o_ref, h_ref):
    @pl.when(pl.program_id(1) == 0)
    def _():
        merged = (jax.nn.sigmoid(g0_ref[...]) * ya_ref[...] + jax.nn.sigmoid(g1_ref[...]) * yb_ref[...]
                  + jax.nn.sigmoid(g2_ref[...]) * yc_ref[...])
        h_ref[...] = merged.astype(BF16)
    o_ref[...] = x_ref[...] + jnp.dot(h_ref[...], w_ref[...], preferred_element_type=F32)


def merge_out(proj, ya, yb, yc, w_out, x, cols, *, tm=256, tn=512):
    M, D = x.shape
    tm = _tile(M, tm)
    tn = _tile(D, tn, LANES)
    g0 = cols["gate"] // D
    assert cols["gate"] % D == 0

    def gate(k):
        return pl.BlockSpec((tm, D), lambda i, j: (i, g0 + k))

    def row():
        return pl.BlockSpec((tm, D), lambda i, j: (i, 0))

    return pl.pallas_call(
        _merge_kernel,
        grid=(M // tm, D // tn),
        in_specs=[gate(0), gate(1), gate(2), row(), row(), row(),
                  pl.BlockSpec((D, tn), lambda i, j: (0, j)), pl.BlockSpec((tm, tn), lambda i, j: (i, j))],
        out_specs=pl.BlockSpec((tm, tn), lambda i, j: (i, j)),
        out_shape=jax.ShapeDtypeStruct((M, D), F32),
        scratch_shapes=[pltpu.VMEM((tm, D), BF16)],
        compiler_params=_cparams("parallel", "arbitrary"),
        name="merge_out",
    )(proj, proj, proj, ya, yb, yc, w_out, x)


def _xattn_kernel(q_ref, k_ref, v_ref, o_ref, *, heads, scale):
    d = q_ref.shape[1] // heads
    for h in range(heads):
        sl = slice(h * d, (h + 1) * d)
        q = q_ref[:, sl].astype(BF16)
        s = lax.dot_general(q, k_ref[:, sl].astype(BF16), (((1,), (1,)), ((), ())),
                            preferred_element_type=F32) * scale
        p = jnp.exp(s - jnp.max(s, axis=-1, keepdims=True))
        p = p / jnp.sum(p, axis=-1, keepdims=True)
        o_ref[:, sl] = jnp.dot(p.astype(BF16), v_ref[:, sl].astype(BF16),
                               preferred_element_type=F32).astype(o_ref.dtype)


def cross_attend(q, k, v, k_col, v_col, nseq, L, mem_len, heads, *, tq=512):
    M, W = q.shape
    tq = _tile(L, tq)
    nq = L // tq
    return pl.pallas_call(
        functools.partial(_xattn_kernel, heads=heads, scale=(W // heads) ** -0.5),
        grid=(nseq, nq),
        in_specs=[pl.BlockSpec((tq, W), lambda b, i: (b * nq + i, 0)),
                  pl.BlockSpec((mem_len, W), lambda b, i: (b, k_col)),
                  pl.BlockSpec((mem_len, W), lambda b, i: (b, v_col))],
        out_specs=pl.BlockSpec((tq, W), lambda b, i: (b * nq + i, 0)),
        out_shape=jax.ShapeDtypeStruct((M, W), F32),
        compiler_params=_cparams("parallel", "parallel"),
        name="cross_attend",
    )(q, k, v)


def _ffn_act_kernel(g_ref, u_ref, prev_ref, cw_ref, cb_ref, o_ref, *, seg, carried):
    prev = prev_ref[...]
    if carried:
        prev = jnp.where(pl.program_id(2) == 0, 0.0, prev)
    gate = _causal_conv(g_ref[...], prev, cw_ref[...], cb_ref[...], seg, carried)
    o_ref[...] = (jax.nn.gelu(gate, approximate=True) * u_ref[...]).astype(o_ref.dtype)


def ffn_act(up, nseq, L, state_conv, conv_w, conv_b, *, rows=256, cb=1536):
    M = up.shape[0]
    F = conv_w.shape[1]
    cb = _tile(F, cb, LANES)
    ncb = F // cb
    if state_conv is None:
        R = _tile(L, rows)
        nch = L // R
        in_specs = [pl.BlockSpec((R, cb), lambda b, j, c: (b * nch + c, j)),
                    pl.BlockSpec((R, cb), lambda b, j, c: (b * nch + c, ncb + j)),
                    pl.BlockSpec((SUBLANES, cb),
                                 lambda b, j, c: (jnp.maximum((b * nch + c) * (R // SUBLANES) - 1, 0), j)),
                    pl.BlockSpec((conv_w.shape[0], cb), lambda b, j, c: (0, j)),
                    pl.BlockSpec((1, cb), lambda b, j, c: (0, j))]
        return pl.pallas_call(
            functools.partial(_ffn_act_kernel, seg=R, carried=True),
            grid=(nseq, ncb, nch),
            in_specs=in_specs,
            out_specs=pl.BlockSpec((R, cb), lambda b, j, c: (b * nch + c, j)),
            out_shape=jax.ShapeDtypeStruct((M, F), BF16),
            compiler_params=_cparams("parallel", "parallel", "arbitrary"),
            name="ffn_act_prompt",
        )(up, up, up, conv_w, conv_b)
    R = _tile(M, rows)
    prev = _pad_state_rows(state_conv)
    in_specs = [pl.BlockSpec((R, cb), lambda i, j: (i, j)),
                pl.BlockSpec((R, cb), lambda i, j: (i, ncb + j)),
                pl.BlockSpec((R, cb), lambda i, j: (i, j)),
                pl.BlockSpec((conv_w.shape[0], cb), lambda i, j: (0, j)),
                pl.BlockSpec((1, cb), lambda i, j: (0, j))]
    return pl.pallas_call(
        functools.partial(_ffn_act_kernel, seg=L, carried=False),
        grid=(M // R, ncb),
        in_specs=in_specs,
        out_specs=pl.BlockSpec((R, cb), lambda i, j: (i, j)),
        out_shape=jax.ShapeDtypeStruct((M, F), BF16),
        compiler_params=_cparams("parallel", "parallel"),
        name="ffn_act_sample",
    )(up, up, prev, conv_w, conv_b)


def _norm_kernel(x_ref, g_ref, o_ref):
    o_ref[...] = _rms(x_ref[...], g_ref[...])


def rms_norm_rows(x, g, *, tm=512):
    M, D = x.shape
    tm = _tile(M, tm)
    return pl.pallas_call(
        _norm_kernel,
        grid=(M // tm,),
        in_specs=[pl.BlockSpec((tm, D), lambda i: (i, 0)), pl.BlockSpec((1, D), lambda i: (0, 0))],
        out_specs=pl.BlockSpec((tm, D), lambda i: (i, 0)),
        out_shape=jax.ShapeDtypeStruct((M, D), F32),
        compiler_params=_cparams("parallel"),
        name="final_norm",
    )(x, g.reshape(1, D))


def _pad_cols(w, width):
    return jnp.pad(w, ((0, 0), (0, width - w.shape[1])))


def _rotate_half_cols(w):
    half = w.shape[-1] // 2
    return jnp.concatenate([-w[..., half:], w[..., :half]], axis=-1)


def _row(v, width=None):
    v = v.reshape(1, -1).astype(F32)
    return v if width is None else _pad_cols(v, width)


def _layer_weights(l, dims, w_in, p):
    D, W, qr, R, rope, inner, cdim, heads_s = (dims[k] for k in
                                                ("D", "lru", "q_rank", "kv_rank", "rope", "inner", "cdim", "ssd_heads"))
    splits = (W, qr, R, rope, inner, cdim, heads_s, 3 * D)
    offs = [0]
    for s in splits:
        offs.append(offs[-1] + s)
    w = w_in[l]
    w_lru, w_qc, w_ckv, w_kr, w_z, w_xbc, w_dt, w_gate = (w[:, offs[i]:offs[i + 1]] for i in range(8))
    small = jnp.concatenate([_pad_cols(w_kr, LANES), _pad_cols(_rotate_half_cols(w_kr), LANES),
                             _pad_cols(w_dt, 2 * LANES)], axis=1)
    assert small.shape[1] == PROJ_BLOCK
    w_main = jnp.concatenate([w_lru, w_z, w_xbc, w_gate, w_qc, w_ckv, small], axis=1).astype(BF16)
    cols, o = {}, 0
    for name, width in (("lru", W), ("z", inner), ("xbc", cdim), ("gate", 3 * D), ("qc", qr), ("ckv", R),
                        ("small", PROJ_BLOCK)):
        assert o % PROJ_BLOCK == 0
        cols[name] = o
        o += width

    H, nope = dims["heads"], dims["nope"]
    hw = nope + LANES
    wq = p["mla_w_qb"][l].reshape(qr, H, nope + rope)
    wq_n, wq_r = wq[..., :nope], wq[..., nope:]
    zpad = jnp.zeros((qr, H, LANES - rope), F32)
    wq_a = jnp.concatenate([wq_n, wq_r, zpad], axis=-1).reshape(qr, H * hw)
    wq_b = jnp.concatenate([jnp.zeros_like(wq_n), _rotate_half_cols(wq_r), zpad], axis=-1).reshape(qr, H * hw)
    w_uk = p["mla_w_uk"][l]
    w_uv = p["mla_w_uv"][l]
    wts = {
        "w_main": w_main,
        "norm_mix": p["norm_mix"][l],
        "lru_conv_w": p["lru_conv_w"][l], "lru_conv_b": _row(p["lru_conv_b"][l]),
        "lru_w_a": p["lru_w_a"][l].astype(BF16), "lru_w_x": p["lru_w_x"][l].astype(BF16),
        "lru_b_a": _row(p["lru_b_a"][l]), "lru_b_x": _row(p["lru_b_x"][l]), "lru_lambda": _row(p["lru_lambda"][l]),
        "mla_q_norm": p["mla_q_norm"][l], "mla_kv_norm": p["mla_kv_norm"][l],
        "wq": jnp.concatenate([wq_a, wq_b], axis=1).astype(BF16),
        "w_uk_flat": w_uk.reshape(R, H * nope).astype(BF16),
        "w_uv_flat": w_uv.reshape(R, -1).astype(BF16),
        "w_uk_heads": jnp.transpose(w_uk, (1, 2, 0)).astype(BF16),
        "w_uv_heads": jnp.transpose(w_uv, (1, 0, 2)).astype(BF16),
        "ssd_heads": heads_s,
        "ssd_conv_w": p["ssd_conv_w"][l], "ssd_conv_b": _row(p["ssd_conv_b"][l]),
        "ssd_dt_bias": _row(p["ssd_dt_bias"][l], LANES), "ssd_a_log": _row(p["ssd_a_log"][l], LANES),
        "ssd_d": _row(p["ssd_d"][l], LANES), "ssd_norm": _row(p["ssd_norm"][l]),
        "w_out": p["w_out"][l].astype(BF16),
        "norm_xa": p["norm_xa"][l], "norm_mem": p["norm_mem"][l],
        "xa_wq": p["xa_wq"][l].astype(BF16),
        "xa_wkv": jnp.concatenate([p["xa_wk"][l], p["xa_wv"][l]], axis=1).astype(BF16),
        "xa_wo": p["xa_wo"][l].astype(BF16),
        "norm_ffn": p["norm_ffn"][l],
        "ffn_w_up": p["ffn_w_up"][l].astype(BF16), "ffn_w_down": p["ffn_w_down"][l].astype(BF16),
        "ffn_conv_w": p["ffn_conv_w"][l], "ffn_conv_b": _row(p["ffn_conv_b"][l]),
    }
    return wts, cols


def _rope_tables(pos0, L, rope, nope, rows):
    inv = 1.0 / (ROPE_THETA ** (jnp.arange(0, rope, 2, dtype=F32) / rope))
    ang = (pos0 + jnp.arange(L)).astype(F32)[:, None] * inv[None, :]
    cos, sin = jnp.cos(ang), jnp.sin(ang)
    cos2, sin2 = jnp.concatenate([cos, cos], axis=1), jnp.concatenate([sin, sin], axis=1)
    zpad = jnp.zeros((L, LANES - rope), F32)
    ck, sk = jnp.concatenate([cos2, zpad], axis=1), jnp.concatenate([sin2, zpad], axis=1)
    cq = jnp.concatenate([jnp.ones((L, nope), F32), ck], axis=1)
    sq = jnp.concatenate([jnp.zeros((L, nope), F32), sk], axis=1)
    reps = max(rows // L, 1)
    return tuple(jnp.tile(t, (reps, 1)) for t in (cq, sq, ck, sk))


def _trunk_layer(x, nseq, L, wts, cols, dims, tables, state, mem_k, mem_v, mem_cols, paged):
    D, W, R, rope, inner, cdim, F = (dims[k] for k in ("D", "lru", "kv_rank", "rope", "inner", "cdim", "d_ff"))
    H, nope = dims["heads"], dims["nope"]
    hw = nope + LANES
    prompt = state is None
    proj = mm(x, wts["w_main"], gain=wts["norm_mix"], name="in_proj")

    y_a = lru_mixer(proj, nseq, L, None if prompt else state["lru_h"], None if prompt else state["lru_conv"], wts)

    q2 = mm(proj, wts["wq"], gain=wts["mla_q_norm"], a_col=cols["qc"] // dims["q_rank"], name="q_proj")
    q_all, ckv, kr = mla_prep(q2, proj, tables, wts["mla_kv_norm"], cols)
    scale = float(nope + rope) ** -0.5
    if prompt:
        kn = mm(ckv, wts["w_uk_flat"], out_dtype=BF16, name="k_up")
        v = mm(ckv, wts["w_uv_flat"], out_dtype=BF16, name="v_up")
        y_b = mha_prompt(q_all, kn, kr, v, nseq, L, H, scale)
    else:
        q_lat = hmm(q_all, wts["w_uk_heads"], a_stride=hw // nope, out_dtype=BF16, name="q_absorb")
        o_lat = mqa_paged(q_lat, q_all, ckv, kr, paged["pool_c"], paged["pool_k"], paged["layer"],
                          paged["page_table"], H, scale)
        y_b = hmm(o_lat, wts["w_uv_heads"], name="v_up_heads")

    y_c, ssd_h = ssd_mixer(proj, nseq, L, None if prompt else state["ssd_h"],
                           None if prompt else state["ssd_conv"], wts, cols)

    x = merge_out(proj, y_a, y_b, y_c, wts["w_out"], x, cols)

    q = mm(x, wts["xa_wq"], gain=wts["norm_xa"], name="xa_q")
    o = cross_attend(q, mem_k, mem_v, mem_cols[0], mem_cols[1], nseq, L, dims["mem_len"], dims["xa_heads"])
    x = mm(o, wts["xa_wo"], res=x, name="xa_o")

    up = mm(x, wts["ffn_w_up"], gain=wts["norm_ffn"], name="ffn_up")
    act = ffn_act(up, nseq, L, None if prompt else state["ffn_conv"], wts["ffn_conv_w"], wts["ffn_conv_b"])
    x = mm(act, wts["ffn_w_down"], res=x, name="ffn_down")

    def tail(a, c0, width, n):
        return a.reshape(nseq, L, -1)[:, L - n:, c0:c0 + width]

    rows = (ckv.reshape(nseq, L, R), kr.reshape(nseq, L, LANES)[..., :rope],
            y_a.reshape(nseq, L, W)[:, L - 1], tail(proj, cols["lru"], W, 3),
            ssd_h, tail(proj, cols["xbc"], cdim, 3), tail(up, 0, F, 2))
    return x, rows


def kernel(x_prompt, x_sample, cache_ckv, cache_krope, cache_mem_k, cache_mem_v, state_lru_h, state_lru_conv, state_ssd_h, state_ssd_conv, state_ffn_conv, page_table, mem_prompt, norm_mix, w_in, lru_conv_w, lru_conv_b, lru_w_a, lru_b_a, lru_w_x, lru_b_x, lru_lambda, mla_q_norm, mla_w_qb, mla_kv_norm, mla_w_uk, mla_w_uv, ssd_conv_w, ssd_conv_b, ssd_dt_bias, ssd_a_log, ssd_d, ssd_norm, w_out, norm_xa, norm_mem, xa_wq, xa_wk, xa_wv, xa_wo, norm_ffn, ffn_w_up, ffn_conv_w, ffn_conv_b, ffn_w_down, norm_final):
    p = dict(norm_mix=norm_mix, lru_conv_w=lru_conv_w, lru_conv_b=lru_conv_b, lru_w_a=lru_w_a, lru_b_a=lru_b_a,
             lru_w_x=lru_w_x, lru_b_x=lru_b_x, lru_lambda=lru_lambda, mla_q_norm=mla_q_norm, mla_w_qb=mla_w_qb,
             mla_kv_norm=mla_kv_norm, mla_w_uk=mla_w_uk, mla_w_uv=mla_w_uv, ssd_conv_w=ssd_conv_w,
             ssd_conv_b=ssd_conv_b, ssd_dt_bias=ssd_dt_bias, ssd_a_log=ssd_a_log, ssd_d=ssd_d, ssd_norm=ssd_norm,
             w_out=w_out, norm_xa=norm_xa, norm_mem=norm_mem, xa_wq=xa_wq, xa_wk=xa_wk, xa_wv=xa_wv, xa_wo=xa_wo,
             norm_ffn=norm_ffn, ffn_w_up=ffn_w_up, ffn_conv_w=ffn_conv_w, ffn_conv_b=ffn_conv_b,
             ffn_w_down=ffn_w_down)
    depth = w_in.shape[0]
    bp, lp, D = x_prompt.shape
    bs, ls, _ = x_sample.shape
    n_pages = page_table.shape[1]
    page = cache_ckv.shape[2]
    mem_len, xa_heads, xa_hd = cache_mem_k.shape[2:]
    R, H, nope = mla_w_uk.shape[1:]
    dims = dict(D=D, lru=lru_conv_w.shape[-1], q_rank=mla_q_norm.shape[-1], kv_rank=R, rope=cache_krope.shape[-1],
                inner=ssd_norm.shape[-1], cdim=ssd_conv_w.shape[-1], ssd_heads=ssd_a_log.shape[-1],
                d_ff=ffn_conv_w.shape[-1], heads=H, nope=nope, mem_len=mem_len, xa_heads=xa_heads)
    xa_w = xa_heads * xa_hd
    tab_p = _rope_tables(0, lp, dims["rope"], nope, lp)
    tab_s = _rope_tables(n_pages * page, ls, dims["rope"], nope, min(256, bs * ls))

    xp = x_prompt.reshape(bp * lp, D)
    xs = x_sample.reshape(bs * ls, D)
    mem_rows = mem_prompt.reshape(bp * mem_len, D)
    p_rows, s_rows, p_mem_k, p_mem_v = [], [], [], []
    for l in range(depth):
        wts, cols = _layer_weights(l, dims, w_in, p)
        mkv = mm(mem_rows, wts["xa_wkv"], gain=wts["norm_mem"], name="mem_kv")
        p_mem_k.append(mkv[:, :xa_w].reshape(bp, mem_len, xa_heads, xa_hd))
        p_mem_v.append(mkv[:, xa_w:].reshape(bp, mem_len, xa_heads, xa_hd))
        xp, rows = _trunk_layer(xp, bp, lp, wts, cols, dims, tab_p, None, mkv, mkv, (0, 1), None)
        p_rows.append(rows)
        state = dict(lru_h=state_lru_h[l], lru_conv=state_lru_conv[l], ssd_h=state_ssd_h[l],
                     ssd_conv=state_ssd_conv[l], ffn_conv=state_ffn_conv[l])
        paged = dict(pool_c=cache_ckv, pool_k=cache_krope, layer=l, page_table=page_table)
        xs, rows = _trunk_layer(xs, bs, ls, wts, cols, dims, tab_s, state,
                                cache_mem_k[l].reshape(bs * mem_len, xa_w), cache_mem_v[l].reshape(bs * mem_len, xa_w),
                                (0, 0), paged)
        s_rows.append(rows)
    y_prompt = rms_norm_rows(xp, norm_final).reshape(bp, lp, D)
    y_sample = rms_norm_rows(xs, norm_final).reshape(bs, ls, D)
    p_out = [jnp.stack(r) for r in zip(*p_rows)]
    s_out = [jnp.stack(r) for r in zip(*s_rows)]
    return (y_prompt, y_sample, *p_out, jnp.stack(p_mem_k), jnp.stack(p_mem_v), *s_out)
```

```python
import functools

import jax
import jax.numpy as jnp
from jax import lax
from jax.experimental import pallas as pl
from jax.experimental.pallas import tpu as pltpu

F32 = jnp.float32
BF16 = jnp.bfloat16
EPS = 1e-6
LRU_C = 8.0
ROPE_THETA = 10000.0
LANES = 128
SUBLANES = 8
VMEM_LIMIT_BYTES = 56 * 2**20
SSD_CHUNK = 128
SSD_STATE = 128
SSD_HEAD_DIM = 64
LRU_BLOCK = 128
PAGES_PER_STEP = 16
PAGES_PER_GROUP = 16
LOG2_E = 1.4426950408889634
PROJ_BLOCK = 512


def _cparams(*sem):
    return pltpu.CompilerParams(dimension_semantics=sem, vmem_limit_bytes=VMEM_LIMIT_BYTES)


def _tile(n, pref, mult=SUBLANES):
    if n <= pref:
        return n
    t = pref - pref % mult
    while t > mult and n % t:
        t -= mult
    assert n % t == 0, (n, pref, mult)
    return t


def _rows(shape):
    return lax.broadcasted_iota(jnp.int32, shape, 0)


def _cols(shape):
    return lax.broadcasted_iota(jnp.int32, shape, 1)


def _rms(x, g):
    return x * lax.rsqrt(jnp.mean(x * x, axis=-1, keepdims=True) + EPS) * g


def _mm_kernel(*refs, has_gain, has_res, stage):
    it = iter(refs)
    a_ref = next(it)
    g_ref = next(it) if has_gain else None
    w_ref = next(it)
    r_ref = next(it) if has_res else None
    o_ref = next(it)
    h_ref = next(it) if stage else None
    if stage:
        @pl.when(pl.program_id(1) == 0)
        def _():
            a = a_ref[...].astype(F32)
            if has_gain:
                a = _rms(a, g_ref[...])
            h_ref[...] = a.astype(BF16)
        h = h_ref[...]
    else:
        h = a_ref[...]
    acc = jnp.dot(h, w_ref[...], preferred_element_type=F32)
    if has_res:
        acc = acc + r_ref[...]
    o_ref[...] = acc.astype(o_ref.dtype)


def mm(a, w, *, gain=None, res=None, a_col=0, out_dtype=F32, tm=1024, tn=512, name="mm"):
    M = a.shape[0]
    K, N = w.shape
    tm = _tile(M, tm)
    tn = _tile(N, tn, LANES)
    has_gain, has_res = gain is not None, res is not None
    stage = has_gain or a.dtype != BF16
    in_specs = [pl.BlockSpec((tm, K), lambda i, j: (i, a_col))]
    args = [a]
    if has_gain:
        in_specs.append(pl.BlockSpec((1, K), lambda i, j: (0, 0)))
        args.append(gain.reshape(1, K).astype(F32))
    in_specs.append(pl.BlockSpec((K, tn), lambda i, j: (0, j)))
    args.append(w)
    if has_res:
        in_specs.append(pl.BlockSpec((tm, tn), lambda i, j: (i, j)))
        args.append(res)
    return pl.pallas_call(
        functools.partial(_mm_kernel, has_gain=has_gain, has_res=has_res, stage=stage),
        grid=(M // tm, N // tn),
        in_specs=in_specs,
        out_specs=pl.BlockSpec((tm, tn), lambda i, j: (i, j)),
        out_shape=jax.ShapeDtypeStruct((M, N), out_dtype),
        scratch_shapes=[pltpu.VMEM((tm, K), BF16)] if stage else [],
        compiler_params=_cparams("parallel", "arbitrary"),
        name=name,
    )(*args)


def _hmm_kernel(a_ref, w_ref, o_ref):
    o_ref[...] = jnp.dot(a_ref[...].astype(BF16), w_ref[...], preferred_element_type=F32).astype(o_ref.dtype)


def hmm(a, w, *, a_stride=1, out_dtype=F32, tm=1024, name="hmm"):
    M = a.shape[0]
    H, Ka, Nb = w.shape
    tm = _tile(M, tm)
    return pl.pallas_call(
        _hmm_kernel,
        grid=(M // tm, H),
        in_specs=[pl.BlockSpec((tm, Ka), lambda i, h: (i, h * a_stride)),
                  pl.BlockSpec((None, Ka, Nb), lambda i, h: (h, 0, 0))],
        out_specs=pl.BlockSpec((tm, Nb), lambda i, h: (i, h)),
        out_shape=jax.ShapeDtypeStruct((M, H * Nb), out_dtype),
        compiler_params=_cparams("parallel", "parallel"),
        name=name,
    )(a, w)


def _shift_rows(x, s, width, prev, seg, halo):
    R = x.shape[0]
    xr = pltpu.roll(x, s, 0)
    if not halo:
        up = width - 1 - s
        pr = pltpu.roll(prev, (R - up) % R, 0) if up else prev
        return jnp.where(_rows(x.shape) % seg >= s, xr, pr)
    pr = pltpu.roll(prev, s, 0)
    top = jnp.where(_rows(pr.shape) >= s, xr[:SUBLANES], pr)
    if R == SUBLANES:
        return top
    return jnp.concatenate([top, xr[SUBLANES:]], axis=0)


def _causal_conv(x, prev, w, b, seg, halo):
    width = w.shape[0]
    y = x * w[width - 1:width]
    for s in range(1, width):
        y = y + _shift_rows(x, s, width, prev, seg, halo) * w[width - 1 - s:width - s]
    return y + b


def _pad_state_rows(buf):
    nb, wm1, c = buf.shape
    return jnp.pad(buf, ((0, 0), (0, SUBLANES - wm1), (0, 0))).reshape(nb * SUBLANES, c)


def _segment_scan(a, u, seg):
    rid = _rows(a.shape) % seg
    d = 1
    while d < seg:
        ok = rid >= d
        a_sh = pltpu.roll(a, d, 0)
        u_sh = pltpu.roll(u, d, 0)
        u = jnp.where(ok, a * u_sh + u, u)
        a = jnp.where(ok, a * a_sh, a)
        d *= 2
    return a, u


def _segment_cumsum(x, seg):
    rid = _rows(x.shape) % seg
    d = 1
    while d < seg:
        x = jnp.where(rid >= d, x + pltpu.roll(x, d, 0), x)
        d *= 2
    return x


def _lru_kernel(*refs, seg, carried):
    if carried:
        x_ref, prev_ref, cw_ref, cb_ref, wa_ref, wx_ref, ba_ref, bx_ref, lam_ref, y_ref, carry_ref = refs
        h0_ref = None
        first = pl.program_id(2) == 0

        @pl.when(first)
        def _():
            carry_ref[...] = jnp.zeros_like(carry_ref)
    else:
        x_ref, prev_ref, cw_ref, cb_ref, wa_ref, wx_ref, ba_ref, bx_ref, lam_ref, h0_ref, y_ref = refs
    for k in range(x_ref.shape[1] // LRU_BLOCK):
        sl = slice(k * LRU_BLOCK, (k + 1) * LRU_BLOCK)
        x = x_ref[:, sl]
        prev = prev_ref[:, sl]
        if carried:
            prev = jnp.where(first, 0.0, prev)
        xc = _causal_conv(x, prev, cw_ref[:, sl], cb_ref[:, sl], seg, carried)
        xb = xc.astype(BF16)
        r = jax.nn.sigmoid(jnp.dot(xb, wa_ref[k], preferred_element_type=F32) + ba_ref[:, sl])
        i = jax.nn.sigmoid(jnp.dot(xb, wx_ref[k], preferred_element_type=F32) + bx_ref[:, sl])
        lam = lam_ref[:, sl]
        softplus_neg_lam = jnp.maximum(-lam, 0.0) + jnp.log1p(jnp.exp(-jnp.abs(lam)))
        log_a = -LRU_C * r * softplus_neg_lam
        a = jnp.exp(log_a)
        u = jnp.sqrt(-jnp.tanh(log_a) * (a * a + 1.0)) * (i * xc)
        if not carried:
            u = u + a * h0_ref[:, sl]
        a_cum, h = _segment_scan(a, u, seg)
        if carried:
            h = h + a_cum * carry_ref[:, sl]
            carry_ref[:, sl] = h[seg - 1:seg]
        y_ref[:, sl] = h


def lru_mixer(proj, nseq, L, state_h, state_conv, wts, *, cb=512, rows=256):
    M = proj.shape[0]
    W = wts["lru_conv_w"].shape[1]
    cb = _tile(W, cb, LANES)
    nb = cb // LRU_BLOCK
    common = [wts["lru_conv_w"], wts["lru_conv_b"], wts["lru_w_a"], wts["lru_w_x"], wts["lru_b_a"],
              wts["lru_b_x"], wts["lru_lambda"]]
    if state_h is None:
        R = _tile(L, rows)
        nch = L // R

        def vec(n):
            return pl.BlockSpec((n, cb), lambda b, j, c: (0, j))

        def gates():
            return pl.BlockSpec((nb, LRU_BLOCK, LRU_BLOCK), lambda b, j, c: (j, 0, 0))

        in_specs = [
            pl.BlockSpec((R, cb), lambda b, j, c: (b * nch + c, j)),
            pl.BlockSpec((SUBLANES, cb), lambda b, j, c: (jnp.maximum((b * nch + c) * (R // SUBLANES) - 1, 0), j)),
            vec(4), vec(1), gates(), gates(), vec(1), vec(1), vec(1),
        ]
        return pl.pallas_call(
            functools.partial(_lru_kernel, seg=R, carried=True),
            grid=(nseq, W // cb, nch),
            in_specs=in_specs,
            out_specs=pl.BlockSpec((R, cb), lambda b, j, c: (b * nch + c, j)),
            out_shape=jax.ShapeDtypeStruct((M, W), F32),
            scratch_shapes=[pltpu.VMEM((1, cb), F32)],
            compiler_params=_cparams("parallel", "parallel", "arbitrary"),
            name="lru_prompt",
        )(proj, proj, *common)
    R = _tile(M, rows)
    h0_rows = jnp.pad(state_h[:, None, :], ((0, 0), (0, L - 1), (0, 0))).reshape(M, W)
    prev = _pad_state_rows(state_conv)

    def vec(n):
        return pl.BlockSpec((n, cb), lambda i, j: (0, j))

    def tile():
        return pl.BlockSpec((R, cb), lambda i, j: (i, j))

    def gates():
        return pl.BlockSpec((nb, LRU_BLOCK, LRU_BLOCK), lambda i, j: (j, 0, 0))

    in_specs = [tile(), tile(), vec(4), vec(1), gates(), gates(), vec(1), vec(1), vec(1), tile()]
    return pl.pallas_call(
        functools.partial(_lru_kernel, seg=L, carried=False),
        grid=(M // R, W // cb),
        in_specs=in_specs,
        out_specs=tile(),
        out_shape=jax.ShapeDtypeStruct((M, W), F32),
        compiler_params=_cparams("parallel", "parallel"),
        name="lru_sample",
    )(proj, prev, *common, h0_rows)


def _mla_prep_kernel(qa_ref, qb_ref, ckv_ref, kr_ref, cq_ref, sq_ref, ck_ref, sk_ref, g_ref,
                     q_ref, ckvn_ref, krr_ref):
    hw = cq_ref.shape[1]
    cq, sq = cq_ref[...], sq_ref[...]
    for h in range(q_ref.shape[1] // hw):
        sl = slice(h * hw, (h + 1) * hw)
        q_ref[:, sl] = (qa_ref[:, sl] * cq + qb_ref[:, sl] * sq).astype(q_ref.dtype)
    ckvn_ref[...] = _rms(ckv_ref[...], g_ref[...])
    kr = kr_ref[...]
    krr_ref[...] = kr[:, :LANES] * ck_ref[...] + kr[:, LANES:2 * LANES] * sk_ref[...]


def mla_prep(q2, proj, tables, kv_gain, cols, *, tm=256):
    M = q2.shape[0]
    qw = q2.shape[1] // 2
    cq, sq, ck, sk = tables
    tm = _tile(min(M, cq.shape[0]), tm)
    nt = cq.shape[0] // tm
    R = kv_gain.shape[0]

    def tab(w):
        return pl.BlockSpec((tm, w), lambda i: (i % nt, 0))

    return pl.pallas_call(
        _mla_prep_kernel,
        grid=(M // tm,),
        in_specs=[pl.BlockSpec((tm, qw), lambda i: (i, 0)), pl.BlockSpec((tm, qw), lambda i: (i, 1)),
                  pl.BlockSpec((tm, R), lambda i: (i, cols["ckv"] // R)),
                  pl.BlockSpec((tm, PROJ_BLOCK), lambda i: (i, cols["small"] // PROJ_BLOCK)),
                  tab(cq.shape[1]), tab(cq.shape[1]), tab(LANES), tab(LANES),
                  pl.BlockSpec((1, R), lambda i: (0, 0))],
        out_specs=[pl.BlockSpec((tm, qw), lambda i: (i, 0)), pl.BlockSpec((tm, R), lambda i: (i, 0)),
                   pl.BlockSpec((tm, LANES), lambda i: (i, 0))],
        out_shape=[jax.ShapeDtypeStruct((M, qw), BF16), jax.ShapeDtypeStruct((M, R), F32),
                   jax.ShapeDtypeStruct((M, LANES), F32)],
        compiler_params=_cparams("parallel"),
        name="mla_prep",
    )(q2, q2, proj, proj, cq, sq, ck, sk, kv_gain.reshape(1, R))


def _mha_kernel(q_ref, kn_ref, kr_ref, v_ref, o_ref, kcat_ref, vaug_ref, *, tk, nq):
    qi = pl.program_id(2)
    dk = kn_ref.shape[1]
    dv = v_ref.shape[1]

    @pl.when(qi == 0)
    def _():
        kcat_ref[:, :dk] = kn_ref[...]
        kcat_ref[:, dk:] = kr_ref[...].astype(BF16)
        vaug_ref[:, :dv] = v_ref[...]
        vaug_ref[:, dv:] = jnp.ones((vaug_ref.shape[0], vaug_ref.shape[1] - dv), BF16)

    q = q_ref[...]
    dn = (((1,), (1,)), ((), ()))
    for n in range(nq):
        @pl.when(qi == n)
        def _(n=n):
            lo, hi = n * tk, (n + 1) * tk
            sd = lax.dot_general(q, kcat_ref[lo:hi, :], dn, preferred_element_type=F32)
            sd = jnp.where(_cols(sd.shape) <= _rows(sd.shape), sd, -jnp.inf)
            m = jnp.max(sd, axis=-1, keepdims=True)
            if n:
                sp = lax.dot_general(q, kcat_ref[0:lo, :], dn, preferred_element_type=F32)
                m = jnp.maximum(m, jnp.max(sp, axis=-1, keepdims=True))
            acc = jnp.dot(jnp.exp2(sd - m).astype(BF16), vaug_ref[lo:hi, :], preferred_element_type=F32)
            if n:
                acc = acc + jnp.dot(jnp.exp2(sp - m).astype(BF16), vaug_ref[0:lo, :], preferred_element_type=F32)
            o_ref[...] = (acc[:, :dv] / acc[:, dv:2 * dv]).astype(o_ref.dtype)


def mha_prompt(q, kn, kr, v, nseq, L, heads, *, tq=512):
    M = q.shape[0]
    hw = q.shape[1] // heads
    dv = v.shape[1] // heads
    tq = _tile(L, tq)
    nq = L // tq
    return pl.pallas_call(
        functools.partial(_mha_kernel, tk=tq, nq=nq),
        grid=(nseq, heads, nq),
        in_specs=[pl.BlockSpec((tq, hw), lambda b, h, i: (b * nq + i, h)),
                  pl.BlockSpec((L, hw - LANES), lambda b, h, i: (b, h)),
                  pl.BlockSpec((L, LANES), lambda b, h, i: (b, 0)),
                  pl.BlockSpec((L, dv), lambda b, h, i: (b, h))],
        out_specs=pl.BlockSpec((tq, dv), lambda b, h, i: (b * nq + i, h)),
        out_shape=jax.ShapeDtypeStruct((M, heads * dv), F32),
        scratch_shapes=[pltpu.VMEM((L, hw), BF16), pltpu.VMEM((L, 2 * dv), BF16)],
        compiler_params=_cparams("parallel", "parallel", "arbitrary"),
        name="mha_prompt",
    )(q, kn, kr, v)


def _mqa_paged_kernel(pt_ref, ql_ref, qr_ref, *refs, pps, group, rope, heads):
    del pt_ref
    c_refs = refs[:pps]
    k_refs = refs[pps:2 * pps]
    cn_ref, kn_ref, o_ref, m_ref, l_ref, acc_ref, cbuf_ref, kbuf_ref = refs[2 * pps:]
    step = pl.program_id(1)
    page = c_refs[0].shape[0]

    @pl.when(step == 0)
    def _():
        m_ref[...] = jnp.full_like(m_ref, -jnp.inf)
        l_ref[...] = jnp.zeros_like(l_ref)
        acc_ref[...] = jnp.zeros_like(acc_ref)

    ql = ql_ref[...]
    qr = qr_ref[:, LANES:LANES + rope]
    dn = (((1,), (1,)), ((), ()))

    def update(s, c):
        m = m_ref[...]
        m_new = jnp.maximum(m, jnp.max(s, axis=-1, keepdims=True))
        alpha = jnp.exp2(m - m_new)
        p = jnp.exp2(s - m_new)
        m_ref[...] = m_new
        l_ref[...] = alpha * l_ref[...] + jnp.sum(p, axis=-1, keepdims=True)
        acc_ref[...] = alpha * acc_ref[...] + jnp.dot(p.astype(BF16), c, preferred_element_type=F32)

    for k, (c_ref, k_ref) in enumerate(zip(c_refs, k_refs)):
        cbuf_ref[k * page:(k + 1) * page, :] = c_ref[...].astype(BF16)
        kbuf_ref[:, k * page:(k + 1) * page] = k_ref[...].astype(BF16)
    for g0 in range(0, pps, group):
        c = cbuf_ref[g0 * page:(g0 + group) * page, :]
        s = lax.dot_general(ql, c, dn, preferred_element_type=F32)
        s = s + jnp.dot(qr, kbuf_ref[:, g0 * page:(g0 + group) * page], preferred_element_type=F32)
        update(s, c)

    @pl.when(step == pl.num_programs(1) - 1)
    def _():
        c = cn_ref[...].astype(BF16)
        s = lax.dot_general(ql, c, dn, preferred_element_type=F32)
        s = s + lax.dot_general(qr, kn_ref[:, :rope].astype(BF16), dn, preferred_element_type=F32)
        s = jnp.where(_cols(s.shape) <= _rows(s.shape) // heads, s, -jnp.inf)
        update(s, c)
        o_ref[...] = (acc_ref[...] / l_ref[...]).astype(o_ref.dtype)


def mqa_paged(q_lat, q_all, ckv_new, kr_new, pool_c, pool_kt, layer, page_table, heads):
    nb, n_pages = page_table.shape
    T = ckv_new.shape[0] // nb
    R = pool_c.shape[-1]
    rope = pool_kt.shape[2]
    page = pool_c.shape[2]
    pps = _tile(n_pages, PAGES_PER_STEP, 1)
    group = _tile(pps, PAGES_PER_GROUP, 1)
    rows = T * heads
    ql = q_lat.reshape(nb * rows, R)
    qa = q_all.reshape(nb * rows, q_all.shape[1] // heads)

    def page_spec(shape, k):
        return pl.BlockSpec((None, None) + shape, lambda b, s, pt: (layer, pt[b, s * pps + k], 0, 0))

    grid_spec = pltpu.PrefetchScalarGridSpec(
        num_scalar_prefetch=1,
        grid=(nb, n_pages // pps),
        in_specs=[pl.BlockSpec((rows, R), lambda b, s, pt: (b, 0)),
                  pl.BlockSpec((rows, qa.shape[1]), lambda b, s, pt: (b, 0))]
                 + [page_spec((page, R), k) for k in range(pps)]
                 + [page_spec((rope, page), k) for k in range(pps)]
                 + [pl.BlockSpec((T, R), lambda b, s, pt: (b, 0)),
                    pl.BlockSpec((T, LANES), lambda b, s, pt: (b, 0))],
        out_specs=pl.BlockSpec((rows, R), lambda b, s, pt: (b, 0)),
        scratch_shapes=[pltpu.VMEM((rows, 1), F32), pltpu.VMEM((rows, 1), F32), pltpu.VMEM((rows, R), F32),
                        pltpu.VMEM((pps * page, R), BF16), pltpu.VMEM((rope, pps * page), BF16)],
    )
    out = pl.pallas_call(
        functools.partial(_mqa_paged_kernel, pps=pps, group=group, rope=rope, heads=heads),
        grid_spec=grid_spec,
        out_shape=jax.ShapeDtypeStruct((nb * rows, R), BF16),
        compiler_params=_cparams("parallel", "arbitrary"),
        name="mqa_paged",
    )(page_table, ql, qa, *([pool_c] * pps), *([pool_kt] * pps), ckv_new, kr_new)
    return out.reshape(nb * T, heads * R)


def _ssd_kernel(*refs, seg, carried, aliased, heads, conv_cb):
    if carried:
        (z_ref, xbc_ref, prev_ref, dt_ref, cw_ref, cb_ref, dtb_ref, alog_ref, dskx_ref, ng_ref, e_ref,
         y_ref, hout_ref, xc_ref, xd_ref, xdw_ref, yacc_ref, yoff_ref, al_ref, st_ref) = refs
        h0_ref = None
        first = pl.program_id(1) == 0

        @pl.when(first)
        def _():
            st_ref[...] = jnp.zeros_like(st_ref)
    else:
        refs = refs[:12] + refs[12 + int(aliased):]
        (z_ref, xbc_ref, prev_ref, dt_ref, cw_ref, cb_ref, dtb_ref, alog_ref, dskx_ref, ng_ref, e_ref, h0_ref,
         y_ref, hout_ref, xc_ref, xd_ref, xdw_ref, yacc_ref, yoff_ref, al_ref) = refs
    R = z_ref.shape[0]
    inner = z_ref.shape[1]
    P = inner // heads
    N = SSD_STATE
    G = (xbc_ref.shape[1] - inner) // (2 * N)
    hpg = heads // G
    gw = hpg * P
    dn_t = (((1,), (1,)), ((), ()))
    exact = lax.Precision.HIGHEST

    for k in range(xbc_ref.shape[1] // conv_cb):
        sl = slice(k * conv_cb, (k + 1) * conv_cb)
        prev = prev_ref[:, sl]
        if carried:
            prev = jnp.where(first, 0.0, prev)
        xc_ref[:, sl] = jax.nn.silu(
            _causal_conv(xbc_ref[:, sl], prev, cw_ref[:, sl], cb_ref[:, sl], seg, carried))

    dt = jax.nn.softplus(dt_ref[...] + dtb_ref[...])
    acum = _segment_cumsum(dt * -jnp.exp(alog_ref[...]), seg)
    rid, cid = _rows((R, R)), _cols((R, R))
    seg_last = (cid == (rid // seg) * seg + (seg - 1)).astype(F32)
    alast = jnp.dot(seg_last, acum, precision=exact, preferred_element_type=F32)
    al_ref[...] = alast
    e_mat = e_ref[...]

    def per_channel(v):
        return jnp.dot(v, e_mat, precision=exact, preferred_element_type=F32)

    xd = xc_ref[:, :inner] * per_channel(dt)
    xd_ref[...] = xd
    xdw_ref[...] = xd * per_channel(jnp.exp(alast - acum))

    if R < LANES:
        acum_sq = jnp.concatenate([acum, jnp.zeros((LANES - R, LANES), F32)], axis=0)
    else:
        acum_sq = acum
    acum_t = acum_sq.T
    allowed = (cid <= rid) & (cid // seg == rid // seg)

    for g in range(G):
        bm = xc_ref[:, inner + g * N: inner + (g + 1) * N].astype(BF16)
        cm = xc_ref[:, inner + (G + g) * N: inner + (G + g + 1) * N].astype(BF16)
        cb_mat = lax.dot_general(cm, bm, dn_t, preferred_element_type=F32)
        for e in range(hpg):
            h = g * hpg + e
            decay = jnp.exp(jnp.where(allowed, acum[:, h:h + 1] - acum_t[h:h + 1, :R], -jnp.inf))
            yacc_ref[:, h * P:(h + 1) * P] = jnp.dot((cb_mat * decay).astype(BF16),
                                                     xd_ref[:, h * P:(h + 1) * P].astype(BF16),
                                                     preferred_element_type=F32)

    def per_segment(s, carry):
        rows = pl.ds(pl.multiple_of(s * seg, seg), seg)
        grow = jnp.exp(al_ref[pl.ds(pl.multiple_of(s * seg, seg), 1), :])
        for g in range(G):
            gs = slice(g * gw, (g + 1) * gw)
            bm = xc_ref[rows, inner + g * N: inner + (g + 1) * N].astype(BF16)
            cm = xc_ref[rows, inner + (G + g) * N: inner + (G + g + 1) * N].astype(BF16)
            hprev = st_ref[gs, :] if carried else h0_ref[s, gs, :]
            yoff_ref[rows, gs] = lax.dot_general(cm, hprev.astype(BF16), dn_t, preferred_element_type=F32)
            st = lax.dot_general(xdw_ref[rows, gs].astype(BF16), bm, (((0,), (0,)), ((), ())),
                                 preferred_element_type=F32)
            for e in range(hpg):
                h = g * hpg + e
                hnew = hprev[e * P:(e + 1) * P] * grow[:, h:h + 1] + st[e * P:(e + 1) * P]
                if carried:
                    st_ref[h * P:(h + 1) * P, :] = hnew
                    hout_ref[h * P:(h + 1) * P, :] = hnew
                else:
                    hout_ref[s, h * P:(h + 1) * P, :] = hnew
        return carry

    if R == seg:
        per_segment(0, 0)
    else:
        lax.fori_loop(0, R // seg, per_segment, 0)

    y = yacc_ref[...] + yoff_ref[...] * per_channel(jnp.exp(acum)) + dskx_ref[...] * xc_ref[:, :inner]
    y_ref[...] = _rms(y * jax.nn.silu(z_ref[...]), ng_ref[...])


def ssd_mixer(proj, nseq, L, state_h, layer, prev_out, state_conv, wts, cols, *, seqs_per_tile=8):
    M = proj.shape[0]
    heads = wts["ssd_heads"]
    P, N = SSD_HEAD_DIM, SSD_STATE
    inner = heads * P
    cdim = wts["ssd_conv_w"].shape[1]
    conv_cb = PROJ_BLOCK
    zc, xc, dc = cols["z"] // inner, cols["xbc"] // cdim, (cols["small"] + 2 * LANES) // LANES
    assert cols["z"] % inner == 0 and cols["xbc"] % cdim == 0
    e_mat = (jnp.arange(LANES)[:, None] == jnp.arange(inner)[None, :] // P).astype(F32)
    vecs = [wts["ssd_conv_w"], wts["ssd_conv_b"], wts["ssd_dt_bias"], wts["ssd_a_log"], wts["ssd_d_x"],
            wts["ssd_norm"], e_mat]
    widths = [(4, cdim), (1, cdim), (1, LANES), (1, LANES), (1, inner), (1, inner), (LANES, inner)]

    def scratch(R):
        return [pltpu.VMEM((R, cdim), F32), pltpu.VMEM((R, inner), F32), pltpu.VMEM((R, inner), F32),
                pltpu.VMEM((R, inner), F32), pltpu.VMEM((R, inner), F32), pltpu.VMEM((R, LANES), F32)]

    if state_h is None:
        Q = SSD_CHUNK if L % SSD_CHUNK == 0 else L
        nch = L // Q
        in_specs = [pl.BlockSpec((Q, inner), lambda b, c: (b * nch + c, zc)),
                    pl.BlockSpec((Q, cdim), lambda b, c: (b * nch + c, xc)),
                    pl.BlockSpec((SUBLANES, cdim),
                                 lambda b, c: (jnp.maximum((b * nch + c) * (Q // SUBLANES) - 1, 0), xc)),
                    pl.BlockSpec((Q, LANES), lambda b, c: (b * nch + c, dc))]
        in_specs += [pl.BlockSpec(w, lambda b, c: (0, 0)) for w in widths]
        y, st = pl.pallas_call(
            functools.partial(_ssd_kernel, seg=Q, carried=True, aliased=False, heads=heads, conv_cb=conv_cb),
            grid=(nseq, nch),
            in_specs=in_specs,
            out_specs=[pl.BlockSpec((Q, inner), lambda b, c: (b * nch + c, 0)),
                       pl.BlockSpec((None, inner, N), lambda b, c: (b, 0, 0))],
            out_shape=[jax.ShapeDtypeStruct((M, inner), F32), jax.ShapeDtypeStruct((nseq, inner, N), F32)],
            scratch_shapes=scratch(Q) + [pltpu.VMEM((inner, N), F32)],
            compiler_params=_cparams("parallel", "arbitrary"),
            name="ssd_prompt",
        )(proj, proj, proj, proj, *vecs)
        return y, st.reshape(nseq, heads, P, N)
    depth = state_h.shape[0]
    ns = _tile(nseq, seqs_per_tile, 1)
    R = ns * L
    prev = _pad_state_rows(state_conv)
    in_specs = [pl.BlockSpec((R, inner), lambda i: (i, zc)),
                pl.BlockSpec((R, cdim), lambda i: (i, xc)),
                pl.BlockSpec((R, cdim), lambda i: (i, 0)),
                pl.BlockSpec((R, LANES), lambda i: (i, dc))]
    in_specs += [pl.BlockSpec(w, lambda i: (0, 0)) for w in widths]
    in_specs += [pl.BlockSpec((None, ns, inner, N), lambda i: (layer, i, 0, 0))]
    args = [proj, proj, prev, proj, *vecs, state_h.reshape(depth, nseq, inner, N)]
    aliases = {}
    if prev_out is not None:
        in_specs.append(pl.BlockSpec(memory_space=pl.ANY))
        args.append(prev_out.reshape(depth, nseq, inner, N))
        aliases = {len(args) - 1: 1}
    y, st = pl.pallas_call(
        functools.partial(_ssd_kernel, seg=L, carried=False, aliased=prev_out is not None, heads=heads,
                          conv_cb=conv_cb),
        grid=(M // R,),
        in_specs=in_specs,
        out_specs=[pl.BlockSpec((R, inner), lambda i: (i, 0)),
                   pl.BlockSpec((None, ns, inner, N), lambda i: (layer, i, 0, 0))],
        out_shape=[jax.ShapeDtypeStruct((M, inner), F32), jax.ShapeDtypeStruct((depth, nseq, inner, N), F32)],
        scratch_shapes=scratch(R),
        input_output_aliases=aliases,
        compiler_params=_cparams("parallel"),
        name="ssd_sample",
    )(*args)
    return y, st.reshape(depth, nseq, heads, P, N)


def _merge_kernel(g0_ref, g1_ref, g2_ref, ya_ref, yb_ref, yc_ref, w_ref, x_ref, o_ref, h_ref):
    @pl.when(pl.program_id(1) == 0)
    def _():
        merged = (jax.nn.sigmoid(g0_ref[...]) * ya_ref[...] + jax.nn.sigmoid(g1_ref[...]) * yb_ref[...]
                  + jax.nn.sigmoid(g2_ref[...]) * yc_ref[...])
        h_ref[...] = merged.astype(BF16)
    o_ref[...] = x_ref[...] + jnp.dot(h_ref[...], w_ref[...], preferred_element_type=F32)


def merge_out(proj, ya, yb, yc, w_out, x, cols, *, tm=256, tn=2048):
    M, D = x.shape
    tm = _tile(M, tm)
    tn = _tile(D, tn, LANES)
    g0 = cols["gate"] // D
    assert cols["gate"] % D == 0

    def gate(k):
        return pl.BlockSpec((tm, D), lambda i, j: (i, g0 + k))

    def row():
        return pl.BlockSpec((tm, D), lambda i, j: (i, 0))

    return pl.pallas_call(
        _merge_kernel,
        grid=(M // tm, D // tn),
        in_specs=[gate(0), gate(1), gate(2), row(), row(), row(),
                  pl.BlockSpec((D, tn), lambda i, j: (0, j)), pl.BlockSpec((tm, tn), lambda i, j: (i, j))],
        out_specs=pl.BlockSpec((tm, tn), lambda i, j: (i, j)),
        out_shape=jax.ShapeDtypeStruct((M, D), F32),
        scratch_shapes=[pltpu.VMEM((tm, D), BF16)],
        compiler_params=_cparams("parallel", "arbitrary"),
        name="merge_out",
    )(proj, proj, proj, ya, yb, yc, w_out, x)


def _xattn_kernel(q_ref, k_ref, v_ref, o_ref, *, heads, scale, seqs):
    d = q_ref.shape[1] // heads
    rows = q_ref.shape[0] // max(seqs, 1)
    for s in range(max(seqs, 1)):
        rs = slice(s * rows, (s + 1) * rows)
        for h in range(heads):
            sl = slice(h * d, (h + 1) * d)
            q = q_ref[rs, sl].astype(BF16)
            k = k_ref[s, :, h, :] if seqs else k_ref[:, sl]
            v = v_ref[s, :, h, :] if seqs else v_ref[:, sl]
            sc = lax.dot_general(q, k.astype(BF16), (((1,), (1,)), ((), ())), preferred_element_type=F32) * scale
            p = jnp.exp(sc - jnp.max(sc, axis=-1, keepdims=True))
            p = p / jnp.sum(p, axis=-1, keepdims=True)
            o_ref[rs, sl] = jnp.dot(p.astype(BF16), v.astype(BF16), preferred_element_type=F32).astype(o_ref.dtype)


def cross_attend_cached(q, mem_k, mem_v, layer, L, *, seqs=4):
    M, W = q.shape
    _, nseq, mem_len, heads, d = mem_k.shape
    seqs = _tile(nseq, seqs, 1)

    def mem():
        return pl.BlockSpec((None, seqs, mem_len, heads, d), lambda i: (layer, i, 0, 0, 0))

    return pl.pallas_call(
        functools.partial(_xattn_kernel, heads=heads, scale=d ** -0.5, seqs=seqs),
        grid=(nseq // seqs,),
        in_specs=[pl.BlockSpec((seqs * L, W), lambda i: (i, 0)), mem(), mem()],
        out_specs=pl.BlockSpec((seqs * L, W), lambda i: (i, 0)),
        out_shape=jax.ShapeDtypeStruct((M, W), F32),
        compiler_params=_cparams("parallel"),
        name="cross_attend_cached",
    )(q, mem_k, mem_v)


def cross_attend(q, k, v, k_col, v_col, nseq, L, mem_len, heads, *, tq=512):
    M, W = q.shape
    tq = _tile(L, tq)
    nq = L // tq
    return pl.pallas_call(
        functools.partial(_xattn_kernel, heads=heads, scale=(W // heads) ** -0.5, seqs=0),
        grid=(nseq, nq),
        in_specs=[pl.BlockSpec((tq, W), lambda b, i: (b * nq + i, 0)),
                  pl.BlockSpec((mem_len, W), lambda b, i: (b, k_col)),
                  pl.BlockSpec((mem_len, W), lambda b, i: (b, v_col))],
        out_specs=pl.BlockSpec((tq, W), lambda b, i: (b * nq + i, 0)),
        out_shape=jax.ShapeDtypeStruct((M, W), F32),
        compiler_params=_cparams("parallel", "parallel"),
        name="cross_attend",
    )(q, k, v)


def _ffn_act_kernel(g_ref, u_ref, prev_ref, cw_ref, cb_ref, o_ref, *, seg, carried):
    prev = prev_ref[...]
    if carried:
        prev = jnp.where(pl.program_id(2) == 0, 0.0, prev)
    gate = _causal_conv(g_ref[...], prev, cw_ref[...], cb_ref[...], seg, carried)
    o_ref[...] = (jax.nn.gelu(gate, approximate=True) * u_ref[...]).astype(o_ref.dtype)


def ffn_act(up, nseq, L, state_conv, conv_w, conv_b, *, rows=256, cb=1536):
    M = up.shape[0]
    F = conv_w.shape[1]
    cb = _tile(F, cb, LANES)
    ncb = F // cb
    if state_conv is None:
        R = _tile(L, rows)
        nch = L // R
        in_specs = [pl.BlockSpec((R, cb), lambda b, j, c: (b * nch + c, j)),
                    pl.BlockSpec((R, cb), lambda b, j, c: (b * nch + c, ncb + j)),
                    pl.BlockSpec((SUBLANES, cb),
                                 lambda b, j, c: (jnp.maximum((b * nch + c) * (R // SUBLANES) - 1, 0), j)),
                    pl.BlockSpec((conv_w.shape[0], cb), lambda b, j, c: (0, j)),
                    pl.BlockSpec((1, cb), lambda b, j, c: (0, j))]
        return pl.pallas_call(
            functools.partial(_ffn_act_kernel, seg=R, carried=True),
            grid=(nseq, ncb, nch),
            in_specs=in_specs,
            out_specs=pl.BlockSpec((R, cb), lambda b, j, c: (b * nch + c, j)),
            out_shape=jax.ShapeDtypeStruct((M, F), BF16),
            compiler_params=_cparams("parallel", "parallel", "arbitrary"),
            name="ffn_act_prompt",
        )(up, up, up, conv_w, conv_b)
    R = _tile(M, rows)
    prev = _pad_state_rows(state_conv)
    in_specs = [pl.BlockSpec((R, cb), lambda i, j: (i, j)),
                pl.BlockSpec((R, cb), lambda i, j: (i, ncb + j)),
                pl.BlockSpec((R, cb), lambda i, j: (i, j)),
                pl.BlockSpec((conv_w.shape[0], cb), lambda i, j: (0, j)),
                pl.BlockSpec((1, cb), lambda i, j: (0, j))]
    return pl.pallas_call(
        functools.partial(_ffn_act_kernel, seg=L, carried=False),
        grid=(M // R, ncb),
        in_specs=in_specs,
        out_specs=pl.BlockSpec((R, cb), lambda i, j: (i, j)),
        out_shape=jax.ShapeDtypeStruct((M, F), BF16),
        compiler_params=_cparams("parallel", "parallel"),
        name="ffn_act_sample",
    )(up, up, prev, conv_w, conv_b)


def _norm_kernel(x_ref, g_ref, o_ref):
    o_ref[...] = _rms(x_ref[...], g_ref[...])


def rms_norm_rows(x, g, *, tm=512):
    M, D = x.shape
    tm = _tile(M, tm)
    return pl.pallas_call(
        _norm_kernel,
        grid=(M // tm,),
        in_specs=[pl.BlockSpec((tm, D), lambda i: (i, 0)), pl.BlockSpec((1, D), lambda i: (0, 0))],
        out_specs=pl.BlockSpec((tm, D), lambda i: (i, 0)),
        out_shape=jax.ShapeDtypeStruct((M, D), F32),
        compiler_params=_cparams("parallel"),
        name="final_norm",
    )(x, g.reshape(1, D))


def _pad_cols(w, width):
    return jnp.pad(w, ((0, 0), (0, width - w.shape[1])))


def _rotate_half_cols(w):
    half = w.shape[-1] // 2
    return jnp.concatenate([-w[..., half:], w[..., :half]], axis=-1)


def _row(v, width=None):
    v = v.reshape(1, -1).astype(F32)
    return v if width is None else _pad_cols(v, width)


def _layer_weights(l, dims, w_in, p):
    D, W, qr, R, rope, inner, cdim, heads_s = (dims[k] for k in
                                                ("D", "lru", "q_rank", "kv_rank", "rope", "inner", "cdim", "ssd_heads"))
    splits = (W, qr, R, rope, inner, cdim, heads_s, 3 * D)
    offs = [0]
    for s in splits:
        offs.append(offs[-1] + s)
    w = w_in[l]
    w_lru, w_qc, w_ckv, w_kr, w_z, w_xbc, w_dt, w_gate = (w[:, offs[i]:offs[i + 1]] for i in range(8))
    small = jnp.concatenate([_pad_cols(w_kr, LANES), _pad_cols(_rotate_half_cols(w_kr), LANES),
                             _pad_cols(w_dt, 2 * LANES)], axis=1)
    assert small.shape[1] == PROJ_BLOCK
    w_main = jnp.concatenate([w_lru, w_z, w_xbc, w_gate, w_qc, w_ckv, small], axis=1).astype(BF16)
    cols, o = {}, 0
    for name, width in (("lru", W), ("z", inner), ("xbc", cdim), ("gate", 3 * D), ("qc", qr), ("ckv", R),
                        ("small", PROJ_BLOCK)):
        assert o % PROJ_BLOCK == 0
        cols[name] = o
        o += width

    H, nope = dims["heads"], dims["nope"]
    hw = nope + LANES
    wq = p["mla_w_qb"][l].reshape(qr, H, nope + rope)
    wq_n, wq_r = wq[..., :nope], wq[..., nope:]
    zpad = jnp.zeros((qr, H, LANES - rope), F32)
    wq_a = jnp.concatenate([wq_n, wq_r, zpad], axis=-1).reshape(qr, H * hw)
    wq_b = jnp.concatenate([jnp.zeros_like(wq_n), _rotate_half_cols(wq_r), zpad], axis=-1).reshape(qr, H * hw)
    w_uk = p["mla_w_uk"][l]
    w_uv = p["mla_w_uv"][l]
    wts = {
        "w_main": w_main,
        "norm_mix": p["norm_mix"][l],
        "lru_conv_w": p["lru_conv_w"][l], "lru_conv_b": _row(p["lru_conv_b"][l]),
        "lru_w_a": p["lru_w_a"][l].astype(BF16), "lru_w_x": p["lru_w_x"][l].astype(BF16),
        "lru_b_a": _row(p["lru_b_a"][l]), "lru_b_x": _row(p["lru_b_x"][l]), "lru_lambda": _row(p["lru_lambda"][l]),
        "mla_q_norm": p["mla_q_norm"][l], "mla_kv_norm": p["mla_kv_norm"][l],
        "wq": jnp.concatenate([wq_a, wq_b], axis=1).astype(BF16),
        "w_uk_flat": w_uk.reshape(R, H * nope).astype(BF16),
        "w_uv_flat": w_uv.reshape(R, -1).astype(BF16),
        "w_uk_heads": jnp.transpose(w_uk, (1, 2, 0)).astype(BF16),
        "w_uv_heads": jnp.transpose(w_uv, (1, 0, 2)).astype(BF16),
        "ssd_heads": heads_s,
        "ssd_conv_w": p["ssd_conv_w"][l], "ssd_conv_b": _row(p["ssd_conv_b"][l]),
        "ssd_dt_bias": _row(p["ssd_dt_bias"][l], LANES), "ssd_a_log": _row(p["ssd_a_log"][l], LANES),
        "ssd_d_x": _row(jnp.repeat(p["ssd_d"][l], SSD_HEAD_DIM)), "ssd_norm": _row(p["ssd_norm"][l]),
        "w_out": p["w_out"][l].astype(BF16),
        "norm_xa": p["norm_xa"][l], "norm_mem": p["norm_mem"][l],
        "xa_wq": p["xa_wq"][l].astype(BF16),
        "xa_wkv": jnp.concatenate([p["xa_wk"][l], p["xa_wv"][l]], axis=1).astype(BF16),
        "xa_wo": p["xa_wo"][l].astype(BF16),
        "norm_ffn": p["norm_ffn"][l],
        "ffn_w_up": p["ffn_w_up"][l].astype(BF16), "ffn_w_down": p["ffn_w_down"][l].astype(BF16),
        "ffn_conv_w": p["ffn_conv_w"][l], "ffn_conv_b": _row(p["ffn_conv_b"][l]),
    }
    return wts, cols


def _rope_tables(pos0, L, rope, nope, rows, qscale):
    inv = 1.0 / (ROPE_THETA ** (jnp.arange(0, rope, 2, dtype=F32) / rope))
    ang = (pos0 + jnp.arange(L)).astype(F32)[:, None] * inv[None, :]
    cos, sin = jnp.cos(ang), jnp.sin(ang)
    cos2, sin2 = jnp.concatenate([cos, cos], axis=1), jnp.concatenate([sin, sin], axis=1)
    zpad = jnp.zeros((L, LANES - rope), F32)
    ck, sk = jnp.concatenate([cos2, zpad], axis=1), jnp.concatenate([sin2, zpad], axis=1)
    cq = jnp.concatenate([jnp.ones((L, nope), F32), ck], axis=1) * qscale
    sq = jnp.concatenate([jnp.zeros((L, nope), F32), sk], axis=1) * qscale
    reps = max(rows // L, 1)
    return tuple(jnp.tile(t, (reps, 1)) for t in (cq, sq, ck, sk))


def _trunk_layer(x, nseq, L, wts, cols, dims, tables, state, mem, paged):
    D, W, R, rope, inner, cdim, F = (dims[k] for k in ("D", "lru", "kv_rank", "rope", "inner", "cdim", "d_ff"))
    H, nope = dims["heads"], dims["nope"]
    hw = nope + LANES
    prompt = state is None
    proj = mm(x, wts["w_main"], gain=wts["norm_mix"], name="in_proj")

    y_a = lru_mixer(proj, nseq, L, None if prompt else state["lru_h"], None if prompt else state["lru_conv"], wts)

    q2 = mm(proj, wts["wq"], gain=wts["mla_q_norm"], a_col=cols["qc"] // dims["q_rank"], name="q_proj")
    q_all, ckv, kr = mla_prep(q2, proj, tables, wts["mla_kv_norm"], cols)
    if prompt:
        kn = mm(ckv, wts["w_uk_flat"], out_dtype=BF16, name="k_up")
        v = mm(ckv, wts["w_uv_flat"], out_dtype=BF16, name="v_up")
        y_b = mha_prompt(q_all, kn, kr, v, nseq, L, H)
    else:
        q_lat = hmm(q_all, wts["w_uk_heads"], a_stride=hw // nope, out_dtype=BF16, name="q_absorb")
        o_lat = mqa_paged(q_lat, q_all, ckv, kr, paged["pool_c"], paged["pool_kt"], paged["layer"],
                          paged["page_table"], H)
        y_b = hmm(o_lat, wts["w_uv_heads"], name="v_up_heads")

    if prompt:
        y_c, ssd_h = ssd_mixer(proj, nseq, L, None, None, None, None, wts, cols)
    else:
        y_c, ssd_h = ssd_mixer(proj, nseq, L, state["ssd_h"], state["layer"], state["ssd_h_out"],
                               state["ssd_conv"], wts, cols)

    x = merge_out(proj, y_a, y_b, y_c, wts["w_out"], x, cols)

    q = mm(x, wts["xa_wq"], gain=wts["norm_xa"], name="xa_q")
    if prompt:
        o = cross_attend(q, mem, mem, 0, 1, nseq, L, dims["mem_len"], dims["xa_heads"])
    else:
        o = cross_attend_cached(q, mem[0], mem[1], mem[2], L)
    x = mm(o, wts["xa_wo"], res=x, name="xa_o")

    up = mm(x, wts["ffn_w_up"], gain=wts["norm_ffn"], name="ffn_up")
    act = ffn_act(up, nseq, L, None if prompt else state["ffn_conv"], wts["ffn_conv_w"], wts["ffn_conv_b"])
    x = mm(act, wts["ffn_w_down"], res=x, name="ffn_down")

    def tail(a, c0, width, n):
        return a.reshape(nseq, L, -1)[:, L - n:, c0:c0 + width]

    rows = (ckv.reshape(nseq, L, R), kr.reshape(nseq, L, LANES)[..., :rope],
            y_a.reshape(nseq, L, W)[:, L - 1], tail(proj, cols["lru"], W, 3),
            ssd_h, tail(proj, cols["xbc"], cdim, 3), tail(up, 0, F, 2))
    return x, rows


def kernel(x_prompt, x_sample, cache_ckv, cache_krope, cache_mem_k, cache_mem_v, state_lru_h, state_lru_conv, state_ssd_h, state_ssd_conv, state_ffn_conv, page_table, mem_prompt, norm_mix, w_in, lru_conv_w, lru_conv_b, lru_w_a, lru_b_a, lru_w_x, lru_b_x, lru_lambda, mla_q_norm, mla_w_qb, mla_kv_norm, mla_w_uk, mla_w_uv, ssd_conv_w, ssd_conv_b, ssd_dt_bias, ssd_a_log, ssd_d, ssd_norm, w_out, norm_xa, norm_mem, xa_wq, xa_wk, xa_wv, xa_wo, norm_ffn, ffn_w_up, ffn_conv_w, ffn_conv_b, ffn_w_down, norm_final):
    p = dict(norm_mix=norm_mix, lru_conv_w=lru_conv_w, lru_conv_b=lru_conv_b, lru_w_a=lru_w_a, lru_b_a=lru_b_a,
             lru_w_x=lru_w_x, lru_b_x=lru_b_x, lru_lambda=lru_lambda, mla_q_norm=mla_q_norm, mla_w_qb=mla_w_qb,
             mla_kv_norm=mla_kv_norm, mla_w_uk=mla_w_uk, mla_w_uv=mla_w_uv, ssd_conv_w=ssd_conv_w,
             ssd_conv_b=ssd_conv_b, ssd_dt_bias=ssd_dt_bias, ssd_a_log=ssd_a_log, ssd_d=ssd_d, ssd_norm=ssd_norm,
             w_out=w_out, norm_xa=norm_xa, norm_mem=norm_mem, xa_wq=xa_wq, xa_wk=xa_wk, xa_wv=xa_wv, xa_wo=xa_wo,
             norm_ffn=norm_ffn, ffn_w_up=ffn_w_up, ffn_conv_w=ffn_conv_w, ffn_conv_b=ffn_conv_b,
             ffn_w_down=ffn_w_down)
    depth = w_in.shape[0]
    bp, lp, D = x_prompt.shape
    bs, ls, _ = x_sample.shape
    n_pages = page_table.shape[1]
    page = cache_ckv.shape[2]
    mem_len, xa_heads, xa_hd = cache_mem_k.shape[2:]
    R, H, nope = mla_w_uk.shape[1:]
    dims = dict(D=D, lru=lru_conv_w.shape[-1], q_rank=mla_q_norm.shape[-1], kv_rank=R, rope=cache_krope.shape[-1],
                inner=ssd_norm.shape[-1], cdim=ssd_conv_w.shape[-1], ssd_heads=ssd_a_log.shape[-1],
                d_ff=ffn_conv_w.shape[-1], heads=H, nope=nope, mem_len=mem_len, xa_heads=xa_heads)
    xa_w = xa_heads * xa_hd
    qscale = float(nope + dims["rope"]) ** -0.5 * LOG2_E
    tab_p = _rope_tables(0, lp, dims["rope"], nope, lp, qscale)
    tab_s = _rope_tables(n_pages * page, ls, dims["rope"], nope, min(256, bs * ls), qscale)
    pool_kt = jnp.swapaxes(cache_krope, 2, 3)

    xp = x_prompt.reshape(bp * lp, D)
    xs = x_sample.reshape(bs * ls, D)
    mem_rows = mem_prompt.reshape(bp * mem_len, D)
    p_rows, s_rows, p_mem_k, p_mem_v = [], [], [], []
    s_ssd_h = None
    for l in range(depth):
        wts, cols = _layer_weights(l, dims, w_in, p)
        mkv = mm(mem_rows, wts["xa_wkv"], gain=wts["norm_mem"], name="mem_kv")
        p_mem_k.append(mkv[:, :xa_w].reshape(bp, mem_len, xa_heads, xa_hd))
        p_mem_v.append(mkv[:, xa_w:].reshape(bp, mem_len, xa_heads, xa_hd))
        xp, rows = _trunk_layer(xp, bp, lp, wts, cols, dims, tab_p, None, mkv, None)
        p_rows.append(rows)
        state = dict(lru_h=state_lru_h[l], lru_conv=state_lru_conv[l], ssd_h=state_ssd_h, layer=l,
                     ssd_h_out=s_ssd_h, ssd_conv=state_ssd_conv[l], ffn_conv=state_ffn_conv[l])
        paged = dict(pool_c=cache_ckv, pool_kt=pool_kt, layer=l, page_table=page_table)
        xs, rows = _trunk_layer(xs, bs, ls, wts, cols, dims, tab_s, state, (cache_mem_k, cache_mem_v, l), paged)
        s_ssd_h = rows[4]
        s_rows.append(rows)
    y_prompt = rms_norm_rows(xp, norm_final).reshape(bp, lp, D)
    y_sample = rms_norm_rows(xs, norm_final).reshape(bs, ls, D)
    p_out = [jnp.stack(r) for r in zip(*p_rows)]
    s_out = [s_ssd_h if i == 4 else jnp.stack(r) for i, r in enumerate(zip(*s_rows))]
    return (y_prompt, y_sample, *p_out, jnp.stack(p_mem_k), jnp.stack(p_mem_v), *s_out)
```

```python
import functools

import jax
import jax.numpy as jnp
from jax import lax
from jax.experimental import pallas as pl
from jax.experimental.pallas import tpu as pltpu

F32 = jnp.float32
BF16 = jnp.bfloat16
EPS = 1e-6
LRU_C = 8.0
ROPE_THETA = 10000.0
LANES = 128
SUBLANES = 8
VMEM_LIMIT_BYTES = 56 * 2**20
SSD_CHUNK = 128
SSD_STATE = 128
SSD_HEAD_DIM = 64
LRU_BLOCK = 128
PAGES_PER_STEP = 32
PAGES_PER_GROUP = 32
LOG2_E = 1.4426950408889634
PROJ_BLOCK = 512


def _cparams(*sem):
    return pltpu.CompilerParams(dimension_semantics=sem, vmem_limit_bytes=VMEM_LIMIT_BYTES)


def _tile(n, pref, mult=SUBLANES):
    if n <= pref:
        return n
    t = pref - pref % mult
    while t > mult and n % t:
        t -= mult
    assert n % t == 0, (n, pref, mult)
    return t


def _rows(shape):
    return lax.broadcasted_iota(jnp.int32, shape, 0)


def _cols(shape):
    return lax.broadcasted_iota(jnp.int32, shape, 1)


def _rms(x, g):
    return x * lax.rsqrt(jnp.mean(x * x, axis=-1, keepdims=True) + EPS) * g


def _mm_kernel(*refs, has_gain, has_res, stage):
    it = iter(refs)
    a_ref = next(it)
    g_ref = next(it) if has_gain else None
    w_ref = next(it)
    r_ref = next(it) if has_res else None
    o_ref = next(it)
    h_ref = next(it) if stage else None
    if stage:
        @pl.when(pl.program_id(1) == 0)
        def _():
            a = a_ref[...].astype(F32)
            if has_gain:
                a = _rms(a, g_ref[...])
            h_ref[...] = a.astype(BF16)
        h = h_ref[...]
    else:
        h = a_ref[...]
    acc = jnp.dot(h, w_ref[...], preferred_element_type=F32)
    if has_res:
        acc = acc + r_ref[...]
    o_ref[...] = acc.astype(o_ref.dtype)


def mm(a, w, *, gain=None, res=None, a_col=0, out_dtype=F32, tm=1024, tn=512, name="mm"):
    M = a.shape[0]
    K, N = w.shape
    tm = _tile(M, tm)
    tn = _tile(N, tn, LANES)
    has_gain, has_res = gain is not None, res is not None
    stage = has_gain or a.dtype != BF16
    in_specs = [pl.BlockSpec((tm, K), lambda i, j: (i, a_col))]
    args = [a]
    if has_gain:
        in_specs.append(pl.BlockSpec((1, K), lambda i, j: (0, 0)))
        args.append(gain.reshape(1, K).astype(F32))
    in_specs.append(pl.BlockSpec((K, tn), lambda i, j: (0, j)))
    args.append(w)
    if has_res:
        in_specs.append(pl.BlockSpec((tm, tn), lambda i, j: (i, j)))
        args.append(res)
    return pl.pallas_call(
        functools.partial(_mm_kernel, has_gain=has_gain, has_res=has_res, stage=stage),
        grid=(M // tm, N // tn),
        in_specs=in_specs,
        out_specs=pl.BlockSpec((tm, tn), lambda i, j: (i, j)),
        out_shape=jax.ShapeDtypeStruct((M, N), out_dtype),
        scratch_shapes=[pltpu.VMEM((tm, K), BF16)] if stage else [],
        compiler_params=_cparams("parallel", "arbitrary"),
        name=name,
    )(*args)


def _hmm_kernel(a_ref, w_ref, o_ref):
    o_ref[...] = jnp.dot(a_ref[...].astype(BF16), w_ref[...], preferred_element_type=F32).astype(o_ref.dtype)


def hmm(a, w, *, a_stride=1, out_dtype=F32, tm=1024, name="hmm"):
    M = a.shape[0]
    H, Ka, Nb = w.shape
    tm = _tile(M, tm)
    return pl.pallas_call(
        _hmm_kernel,
        grid=(M // tm, H),
        in_specs=[pl.BlockSpec((tm, Ka), lambda i, h: (i, h * a_stride)),
                  pl.BlockSpec((None, Ka, Nb), lambda i, h: (h, 0, 0))],
        out_specs=pl.BlockSpec((tm, Nb), lambda i, h: (i, h)),
        out_shape=jax.ShapeDtypeStruct((M, H * Nb), out_dtype),
        compiler_params=_cparams("parallel", "parallel"),
        name=name,
    )(a, w)


def _shift_rows(x, s, width, prev, seg, halo):
    R = x.shape[0]
    xr = pltpu.roll(x, s, 0)
    if not halo:
        up = width - 1 - s
        pr = pltpu.roll(prev, (R - up) % R, 0) if up else prev
        return jnp.where(_rows(x.shape) % seg >= s, xr, pr)
    pr = pltpu.roll(prev, s, 0)
    top = jnp.where(_rows(pr.shape) >= s, xr[:SUBLANES], pr)
    if R == SUBLANES:
        return top
    return jnp.concatenate([top, xr[SUBLANES:]], axis=0)


def _causal_conv(x, prev, w, b, seg, halo):
    width = w.shape[0]
    y = x * w[width - 1:width]
    for s in range(1, width):
        y = y + _shift_rows(x, s, width, prev, seg, halo) * w[width - 1 - s:width - s]
    return y + b


def _pad_state_rows(buf):
    nb, wm1, c = buf.shape
    return jnp.pad(buf, ((0, 0), (0, SUBLANES - wm1), (0, 0))).reshape(nb * SUBLANES, c)


def _segment_scan(a, u, seg):
    rid = _rows(a.shape) % seg
    d = 1
    while d < seg:
        ok = rid >= d
        a_sh = pltpu.roll(a, d, 0)
        u_sh = pltpu.roll(u, d, 0)
        u = jnp.where(ok, a * u_sh + u, u)
        a = jnp.where(ok, a * a_sh, a)
        d *= 2
    return a, u


def _segment_cumsum(x, seg):
    rid = _rows(x.shape) % seg
    d = 1
    while d < seg:
        x = jnp.where(rid >= d, x + pltpu.roll(x, d, 0), x)
        d *= 2
    return x


def _lru_kernel(*refs, seg, carried):
    if carried:
        x_ref, prev_ref, cw_ref, cb_ref, wa_ref, wx_ref, ba_ref, bx_ref, lam_ref, y_ref, carry_ref = refs
        h0_ref = None
        first = pl.program_id(2) == 0

        @pl.when(first)
        def _():
            carry_ref[...] = jnp.zeros_like(carry_ref)
    else:
        x_ref, prev_ref, cw_ref, cb_ref, wa_ref, wx_ref, ba_ref, bx_ref, lam_ref, h0_ref, y_ref = refs
    for k in range(x_ref.shape[1] // LRU_BLOCK):
        sl = slice(k * LRU_BLOCK, (k + 1) * LRU_BLOCK)
        x = x_ref[:, sl]
        prev = prev_ref[:, sl]
        if carried:
            prev = jnp.where(first, 0.0, prev)
        xc = _causal_conv(x, prev, cw_ref[:, sl], cb_ref[:, sl], seg, carried)
        xb = xc.astype(BF16)
        r = jax.nn.sigmoid(jnp.dot(xb, wa_ref[k], preferred_element_type=F32) + ba_ref[:, sl])
        i = jax.nn.sigmoid(jnp.dot(xb, wx_ref[k], preferred_element_type=F32) + bx_ref[:, sl])
        lam = lam_ref[:, sl]
        softplus_neg_lam = jnp.maximum(-lam, 0.0) + jnp.log1p(jnp.exp(-jnp.abs(lam)))
        log_a = -LRU_C * r * softplus_neg_lam
        a = jnp.exp(log_a)
        u = jnp.sqrt(-jnp.tanh(log_a) * (a * a + 1.0)) * (i * xc)
        if not carried:
            u = u + a * h0_ref[:, sl]
            y_ref[:, sl] = _segment_scan(a, u, seg)[1]
        else:
            a8, u8 = _segment_scan(a, u, SUBLANES)
            hprev = carry_ref[:, sl]
            for g in range(seg // SUBLANES):
                gs = slice(g * SUBLANES, (g + 1) * SUBLANES)
                hg = u8[gs] + a8[gs] * hprev
                y_ref[gs, sl] = hg
                hprev = hg[SUBLANES - 1:SUBLANES]
            carry_ref[:, sl] = hprev


def lru_mixer(proj, nseq, L, state_h, state_conv, wts, *, cb=512, rows=256):
    M = proj.shape[0]
    W = wts["lru_conv_w"].shape[1]
    cb = _tile(W, cb, LANES)
    nb = cb // LRU_BLOCK
    common = [wts["lru_conv_w"], wts["lru_conv_b"], wts["lru_w_a"], wts["lru_w_x"], wts["lru_b_a"],
              wts["lru_b_x"], wts["lru_lambda"]]
    if state_h is None:
        R = _tile(L, rows)
        nch = L // R

        def vec(n):
            return pl.BlockSpec((n, cb), lambda b, j, c: (0, j))

        def gates():
            return pl.BlockSpec((nb, LRU_BLOCK, LRU_BLOCK), lambda b, j, c: (j, 0, 0))

        in_specs = [
            pl.BlockSpec((R, cb), lambda b, j, c: (b * nch + c, j)),
            pl.BlockSpec((SUBLANES, cb), lambda b, j, c: (jnp.maximum((b * nch + c) * (R // SUBLANES) - 1, 0), j)),
            vec(4), vec(1), gates(), gates(), vec(1), vec(1), vec(1),
        ]
        return pl.pallas_call(
            functools.partial(_lru_kernel, seg=R, carried=True),
            grid=(nseq, W // cb, nch),
            in_specs=in_specs,
            out_specs=pl.BlockSpec((R, cb), lambda b, j, c: (b * nch + c, j)),
            out_shape=jax.ShapeDtypeStruct((M, W), F32),
            scratch_shapes=[pltpu.VMEM((1, cb), F32)],
            compiler_params=_cparams("parallel", "parallel", "arbitrary"),
            name="lru_prompt",
        )(proj, proj, *common)
    R = _tile(M, rows)
    h0_rows = jnp.pad(state_h[:, None, :], ((0, 0), (0, L - 1), (0, 0))).reshape(M, W)
    prev = _pad_state_rows(state_conv)

    def vec(n):
        return pl.BlockSpec((n, cb), lambda i, j: (0, j))

    def tile():
        return pl.BlockSpec((R, cb), lambda i, j: (i, j))

    def gates():
        return pl.BlockSpec((nb, LRU_BLOCK, LRU_BLOCK), lambda i, j: (j, 0, 0))

    in_specs = [tile(), tile(), vec(4), vec(1), gates(), gates(), vec(1), vec(1), vec(1), tile()]
    return pl.pallas_call(
        functools.partial(_lru_kernel, seg=L, carried=False),
        grid=(M // R, W // cb),
        in_specs=in_specs,
        out_specs=tile(),
        out_shape=jax.ShapeDtypeStruct((M, W), F32),
        compiler_params=_cparams("parallel", "parallel"),
        name="lru_sample",
    )(proj, prev, *common, h0_rows)


def _mla_prep_kernel(qa_ref, qb_ref, ckv_ref, kr_ref, cq_ref, sq_ref, ck_ref, sk_ref, g_ref,
                     q_ref, ckvn_ref, krr_ref):
    hw = cq_ref.shape[1]
    nope = hw - LANES
    cq, sq = cq_ref[...], sq_ref[:, nope:]
    for h in range(q_ref.shape[1] // hw):
        sl = slice(h * hw, (h + 1) * hw)
        qa = qa_ref[:, sl] * cq
        rot = qb_ref[:, h * LANES:(h + 1) * LANES] * sq
        q_ref[:, sl] = jnp.concatenate([qa[:, :nope], qa[:, nope:] + rot], axis=1).astype(q_ref.dtype)
    ckvn_ref[...] = _rms(ckv_ref[...], g_ref[...])
    kr = kr_ref[...]
    krr_ref[...] = kr[:, :LANES] * ck_ref[...] + kr[:, LANES:2 * LANES] * sk_ref[...]


def mla_prep(q2, proj, tables, kv_gain, cols, *, tm=256):
    M = q2.shape[0]
    cq, sq, ck, sk = tables
    hw = cq.shape[1]
    heads = q2.shape[1] // (hw + LANES)
    qw, qbw = heads * hw, heads * LANES
    assert qw % qbw == 0
    tm = _tile(min(M, cq.shape[0]), tm)
    nt = cq.shape[0] // tm
    R = kv_gain.shape[0]

    def tab(w):
        return pl.BlockSpec((tm, w), lambda i: (i % nt, 0))

    return pl.pallas_call(
        _mla_prep_kernel,
        grid=(M // tm,),
        in_specs=[pl.BlockSpec((tm, qw), lambda i: (i, 0)), pl.BlockSpec((tm, qbw), lambda i: (i, qw // qbw)),
                  pl.BlockSpec((tm, R), lambda i: (i, cols["ckv"] // R)),
                  pl.BlockSpec((tm, PROJ_BLOCK), lambda i: (i, cols["small"] // PROJ_BLOCK)),
                  tab(cq.shape[1]), tab(cq.shape[1]), tab(LANES), tab(LANES),
                  pl.BlockSpec((1, R), lambda i: (0, 0))],
        out_specs=[pl.BlockSpec((tm, qw), lambda i: (i, 0)), pl.BlockSpec((tm, R), lambda i: (i, 0)),
                   pl.BlockSpec((tm, LANES), lambda i: (i, 0))],
        out_shape=[jax.ShapeDtypeStruct((M, qw), BF16), jax.ShapeDtypeStruct((M, R), F32),
                   jax.ShapeDtypeStruct((M, LANES), F32)],
        compiler_params=_cparams("parallel"),
        name="mla_prep",
    )(q2, q2, proj, proj, cq, sq, ck, sk, kv_gain.reshape(1, R))


def _mha_kernel(q_ref, kn_ref, kr_ref, v_ref, o_ref, kcat_ref, vaug_ref, *, tk, nq):
    qi = pl.program_id(2)
    dk = kn_ref.shape[1]
    dv = v_ref.shape[1]

    @pl.when(qi == 0)
    def _():
        kcat_ref[:, :dk] = kn_ref[...]
        kcat_ref[:, dk:] = kr_ref[...].astype(BF16)
        vaug_ref[:, :dv] = v_ref[...]
        vaug_ref[:, dv:] = jnp.ones((vaug_ref.shape[0], vaug_ref.shape[1] - dv), BF16)

    q = q_ref[...]
    dn = (((1,), (1,)), ((), ()))
    for n in range(nq):
        @pl.when(qi == n)
        def _(n=n):
            lo, hi = n * tk, (n + 1) * tk
            sd = lax.dot_general(q, kcat_ref[lo:hi, :], dn, preferred_element_type=F32)
            sd = jnp.where(_cols(sd.shape) <= _rows(sd.shape), sd, -jnp.inf)
            m = jnp.max(sd, axis=-1, keepdims=True)
            if n:
                sp = lax.dot_general(q, kcat_ref[0:lo, :], dn, preferred_element_type=F32)
                m = jnp.maximum(m, jnp.max(sp, axis=-1, keepdims=True))
            acc = jnp.dot(jnp.exp2(sd - m).astype(BF16), vaug_ref[lo:hi, :], preferred_element_type=F32)
            if n:
                acc = acc + jnp.dot(jnp.exp2(sp - m).astype(BF16), vaug_ref[0:lo, :], preferred_element_type=F32)
            o_ref[...] = (acc[:, :dv] / acc[:, dv:2 * dv]).astype(o_ref.dtype)


def mha_prompt(q, kn, kr, v, nseq, L, heads, *, tq=512):
    M = q.shape[0]
    hw = q.shape[1] // heads
    dv = v.shape[1] // heads
    tq = _tile(L, tq)
    nq = L // tq
    return pl.pallas_call(
        functools.partial(_mha_kernel, tk=tq, nq=nq),
        grid=(nseq, heads, nq),
        in_specs=[pl.BlockSpec((tq, hw), lambda b, h, i: (b * nq + i, h)),
                  pl.BlockSpec((L, hw - LANES), lambda b, h, i: (b, h)),
                  pl.BlockSpec((L, LANES), lambda b, h, i: (b, 0)),
                  pl.BlockSpec((L, dv), lambda b, h, i: (b, h))],
        out_specs=pl.BlockSpec((tq, dv), lambda b, h, i: (b * nq + i, h)),
        out_shape=jax.ShapeDtypeStruct((M, heads * dv), F32),
        scratch_shapes=[pltpu.VMEM((L, hw), BF16), pltpu.VMEM((L, 2 * dv), BF16)],
        compiler_params=_cparams("parallel", "parallel", "arbitrary"),
        name="mha_prompt",
    )(q, kn, kr, v)


def _mqa_paged_kernel(pt_ref, ql_ref, qr_ref, *refs, pps, group, rope, heads):
    del pt_ref
    c_refs = refs[:pps]
    k_refs = refs[pps:2 * pps]
    cn_ref, kn_ref, o_ref, m_ref, l_ref, acc_ref, cbuf_ref, kbuf_ref = refs[2 * pps:]
    step = pl.program_id(1)
    page = c_refs[0].shape[0]

    @pl.when(step == 0)
    def _():
        m_ref[...] = jnp.full_like(m_ref, -jnp.inf)
        l_ref[...] = jnp.zeros_like(l_ref)
        acc_ref[...] = jnp.zeros_like(acc_ref)

    ql = ql_ref[...]
    qr = qr_ref[:, LANES:LANES + rope]
    dn = (((1,), (1,)), ((), ()))

    def update(s, c):
        m = m_ref[...]
        m_new = jnp.maximum(m, jnp.max(s, axis=-1, keepdims=True))
        alpha = jnp.exp2(m - m_new)
        p = jnp.exp2(s - m_new)
        m_ref[...] = m_new
        l_ref[...] = alpha * l_ref[...] + jnp.sum(p, axis=-1, keepdims=True)
        acc_ref[...] = alpha * acc_ref[...] + jnp.dot(p.astype(BF16), c, preferred_element_type=F32)

    for k, (c_ref, k_ref) in enumerate(zip(c_refs, k_refs)):
        cbuf_ref[k * page:(k + 1) * page, :] = c_ref[...].astype(BF16)
        kbuf_ref[:, k * page:(k + 1) * page] = k_ref[...].astype(BF16)
    for g0 in range(0, pps, group):
        c = cbuf_ref[g0 * page:(g0 + group) * page, :]
        s = lax.dot_general(ql, c, dn, preferred_element_type=F32)
        s = s + jnp.dot(qr, kbuf_ref[:, g0 * page:(g0 + group) * page], preferred_element_type=F32)
        update(s, c)

    @pl.when(step == pl.num_programs(1) - 1)
    def _():
        c = cn_ref[...].astype(BF16)
        s = lax.dot_general(ql, c, dn, preferred_element_type=F32)
        s = s + lax.dot_general(qr, kn_ref[:, :rope].astype(BF16), dn, preferred_element_type=F32)
        s = jnp.where(_cols(s.shape) <= _rows(s.shape) // heads, s, -jnp.inf)
        update(s, c)
        o_ref[...] = (acc_ref[...] / l_ref[...]).astype(o_ref.dtype)


def mqa_paged(q_lat, q_all, ckv_new, kr_new, pool_c, pool_kt, layer, page_table, heads):
    nb, n_pages = page_table.shape
    T = ckv_new.shape[0] // nb
    R = pool_c.shape[-1]
    rope = pool_kt.shape[2]
    page = pool_c.shape[2]
    pps = _tile(n_pages, PAGES_PER_STEP, 1)
    group = _tile(pps, PAGES_PER_GROUP, 1)
    rows = T * heads
    ql = q_lat.reshape(nb * rows, R)
    qa = q_all.reshape(nb * rows, q_all.shape[1] // heads)

    def page_spec(shape, k):
        return pl.BlockSpec((None, None) + shape, lambda b, s, pt: (layer, pt[b, s * pps + k], 0, 0))

    grid_spec = pltpu.PrefetchScalarGridSpec(
        num_scalar_prefetch=1,
        grid=(nb, n_pages // pps),
        in_specs=[pl.BlockSpec((rows, R), lambda b, s, pt: (b, 0)),
                  pl.BlockSpec((rows, qa.shape[1]), lambda b, s, pt: (b, 0))]
                 + [page_spec((page, R), k) for k in range(pps)]
                 + [page_spec((rope, page), k) for k in range(pps)]
                 + [pl.BlockSpec((T, R), lambda b, s, pt: (b, 0)),
                    pl.BlockSpec((T, LANES), lambda b, s, pt: (b, 0))],
        out_specs=pl.BlockSpec((rows, R), lambda b, s, pt: (b, 0)),
        scratch_shapes=[pltpu.VMEM((rows, 1), F32), pltpu.VMEM((rows, 1), F32), pltpu.VMEM((rows, R), F32),
                        pltpu.VMEM((pps * page, R), BF16), pltpu.VMEM((rope, pps * page), BF16)],
    )
    out = pl.pallas_call(
        functools.partial(_mqa_paged_kernel, pps=pps, group=group, rope=rope, heads=heads),
        grid_spec=grid_spec,
        out_shape=jax.ShapeDtypeStruct((nb * rows, R), BF16),
        compiler_params=_cparams("parallel", "arbitrary"),
        name="mqa_paged",
    )(page_table, ql, qa, *([pool_c] * pps), *([pool_kt] * pps), ckv_new, kr_new)
    return out.reshape(nb * T, heads * R)


def _ssd_kernel(*refs, seg, carried, aliased, heads, conv_cb):
    if carried:
        (z_ref, xbc_ref, prev_ref, dt_ref, cw_ref, cb_ref, dtb_ref, alog_ref, dskx_ref, ng_ref, e_ref,
         y_ref, hout_ref, xc_ref, xd_ref, xdw_ref, yacc_ref, yoff_ref, al_ref, st_ref) = refs
        h0_ref = None
        first = pl.program_id(1) == 0

        @pl.when(first)
        def _():
            st_ref[...] = jnp.zeros_like(st_ref)
    else:
        refs = refs[:12] + refs[12 + int(aliased):]
        (z_ref, xbc_ref, prev_ref, dt_ref, cw_ref, cb_ref, dtb_ref, alog_ref, dskx_ref, ng_ref, e_ref, h0_ref,
         y_ref, hout_ref, xc_ref, xd_ref, xdw_ref, yacc_ref, yoff_ref, al_ref) = refs
    R = z_ref.shape[0]
    inner = z_ref.shape[1]
    P = inner // heads
    N = SSD_STATE
    G = (xbc_ref.shape[1] - inner) // (2 * N)
    hpg = heads // G
    gw = hpg * P
    dn_t = (((1,), (1,)), ((), ()))
    exact = lax.Precision.HIGHEST

    for k in range(xbc_ref.shape[1] // conv_cb):
        sl = slice(k * conv_cb, (k + 1) * conv_cb)
        prev = prev_ref[:, sl]
        if carried:
            prev = jnp.where(first, 0.0, prev)
        xc_ref[:, sl] = jax.nn.silu(
            _causal_conv(xbc_ref[:, sl], prev, cw_ref[:, sl], cb_ref[:, sl], seg, carried))

    dt = jax.nn.softplus(dt_ref[...] + dtb_ref[...])
    acum = _segment_cumsum(dt * -jnp.exp(alog_ref[...]), seg)
    rid, cid = _rows((R, R)), _cols((R, R))
    seg_last = (cid == (rid // seg) * seg + (seg - 1)).astype(F32)
    alast = jnp.dot(seg_last, acum, precision=exact, preferred_element_type=F32)
    al_ref[...] = alast
    e_mat = e_ref[...]

    def per_channel(v):
        return jnp.dot(v, e_mat, precision=exact, preferred_element_type=F32)

    xd = xc_ref[:, :inner] * per_channel(dt)
    xd_ref[...] = xd
    xdw_ref[...] = xd * per_channel(jnp.exp(alast - acum))

    if R < LANES:
        acum_sq = jnp.concatenate([acum, jnp.zeros((LANES - R, LANES), F32)], axis=0)
    else:
        acum_sq = acum
    acum_t = acum_sq.T
    allowed = (cid <= rid) & (cid // seg == rid // seg)

    for g in range(G):
        bm = xc_ref[:, inner + g * N: inner + (g + 1) * N].astype(BF16)
        cm = xc_ref[:, inner + (G + g) * N: inner + (G + g + 1) * N].astype(BF16)
        cb_mat = lax.dot_general(cm, bm, dn_t, preferred_element_type=F32)
        for e in range(hpg):
            h = g * hpg + e
            decay = jnp.exp(jnp.where(allowed, acum[:, h:h + 1] - acum_t[h:h + 1, :R], -jnp.inf))
            yacc_ref[:, h * P:(h + 1) * P] = jnp.dot((cb_mat * decay).astype(BF16),
                                                     xd_ref[:, h * P:(h + 1) * P].astype(BF16),
                                                     preferred_element_type=F32)

    def per_segment(s, carry):
        rows = pl.ds(pl.multiple_of(s * seg, seg), seg)
        grow = jnp.exp(al_ref[pl.ds(pl.multiple_of(s * seg, seg), 1), :])
        for g in range(G):
            gs = slice(g * gw, (g + 1) * gw)
            bm = xc_ref[rows, inner + g * N: inner + (g + 1) * N].astype(BF16)
            cm = xc_ref[rows, inner + (G + g) * N: inner + (G + g + 1) * N].astype(BF16)
            hprev = st_ref[gs, :] if carried else h0_ref[s, gs, :]
            yoff_ref[rows, gs] = lax.dot_general(cm, hprev.astype(BF16), dn_t, preferred_element_type=F32)
            st = lax.dot_general(xdw_ref[rows, gs].astype(BF16), bm, (((0,), (0,)), ((), ())),
                                 preferred_element_type=F32)
            for e in range(hpg):
                h = g * hpg + e
                hnew = hprev[e * P:(e + 1) * P] * grow[:, h:h + 1] + st[e * P:(e + 1) * P]
                if carried:
                    st_ref[h * P:(h + 1) * P, :] = hnew
                    hout_ref[h * P:(h + 1) * P, :] = hnew
                else:
                    hout_ref[s, h * P:(h + 1) * P, :] = hnew
        return carry

    if R == seg:
        per_segment(0, 0)
    else:
        lax.fori_loop(0, R // seg, per_segment, 0)

    y = yacc_ref[...] + yoff_ref[...] * per_channel(jnp.exp(acum)) + dskx_ref[...] * xc_ref[:, :inner]
    y_ref[...] = _rms(y * jax.nn.silu(z_ref[...]), ng_ref[...])


def ssd_mixer(proj, nseq, L, state_h, layer, prev_out, state_conv, wts, cols, *, seqs_per_tile=8):
    M = proj.shape[0]
    heads = wts["ssd_heads"]
    P, N = SSD_HEAD_DIM, SSD_STATE
    inner = heads * P
    cdim = wts["ssd_conv_w"].shape[1]
    conv_cb = PROJ_BLOCK
    zc, xc, dc = cols["z"] // inner, cols["xbc"] // cdim, (cols["small"] + 2 * LANES) // LANES
    assert cols["z"] % inner == 0 and cols["xbc"] % cdim == 0
    e_mat = (jnp.arange(LANES)[:, None] == jnp.arange(inner)[None, :] // P).astype(F32)
    vecs = [wts["ssd_conv_w"], wts["ssd_conv_b"], wts["ssd_dt_bias"], wts["ssd_a_log"], wts["ssd_d_x"],
            wts["ssd_norm"], e_mat]
    widths = [(4, cdim), (1, cdim), (1, LANES), (1, LANES), (1, inner), (1, inner), (LANES, inner)]

    def scratch(R):
        return [pltpu.VMEM((R, cdim), F32), pltpu.VMEM((R, inner), F32), pltpu.VMEM((R, inner), F32),
                pltpu.VMEM((R, inner), F32), pltpu.VMEM((R, inner), F32), pltpu.VMEM((R, LANES), F32)]

    if state_h is None:
        Q = SSD_CHUNK if L % SSD_CHUNK == 0 else L
        nch = L // Q
        in_specs = [pl.BlockSpec((Q, inner), lambda b, c: (b * nch + c, zc)),
                    pl.BlockSpec((Q, cdim), lambda b, c: (b * nch + c, xc)),
                    pl.BlockSpec((SUBLANES, cdim),
                                 lambda b, c: (jnp.maximum((b * nch + c) * (Q // SUBLANES) - 1, 0), xc)),
                    pl.BlockSpec((Q, LANES), lambda b, c: (b * nch + c, dc))]
        in_specs += [pl.BlockSpec(w, lambda b, c: (0, 0)) for w in widths]
        y, st = pl.pallas_call(
            functools.partial(_ssd_kernel, seg=Q, carried=True, aliased=False, heads=heads, conv_cb=conv_cb),
            grid=(nseq, nch),
            in_specs=in_specs,
            out_specs=[pl.BlockSpec((Q, inner), lambda b, c: (b * nch + c, 0)),
                       pl.BlockSpec((None, inner, N), lambda b, c: (b, 0, 0))],
            out_shape=[jax.ShapeDtypeStruct((M, inner), F32), jax.ShapeDtypeStruct((nseq, inner, N), F32)],
            scratch_shapes=scratch(Q) + [pltpu.VMEM((inner, N), F32)],
            compiler_params=_cparams("parallel", "arbitrary"),
            name="ssd_prompt",
        )(proj, proj, proj, proj, *vecs)
        return y, st.reshape(nseq, heads, P, N)
    depth = state_h.shape[0]
    ns = _tile(nseq, seqs_per_tile, 1)
    R = ns * L
    prev = _pad_state_rows(state_conv)
    in_specs = [pl.BlockSpec((R, inner), lambda i: (i, zc)),
                pl.BlockSpec((R, cdim), lambda i: (i, xc)),
                pl.BlockSpec((R, cdim), lambda i: (i, 0)),
                pl.BlockSpec((R, LANES), lambda i: (i, dc))]
    in_specs += [pl.BlockSpec(w, lambda i: (0, 0)) for w in widths]
    in_specs += [pl.BlockSpec((None, ns, inner, N), lambda i: (layer, i, 0, 0))]
    args = [proj, proj, prev, proj, *vecs, state_h.reshape(depth, nseq, inner, N)]
    aliases = {}
    if prev_out is not None:
        in_specs.append(pl.BlockSpec(memory_space=pl.ANY))
        args.append(prev_out.reshape(depth, nseq, inner, N))
        aliases = {len(args) - 1: 1}
    y, st = pl.pallas_call(
        functools.partial(_ssd_kernel, seg=L, carried=False, aliased=prev_out is not None, heads=heads,
                          conv_cb=conv_cb),
        grid=(M // R,),
        in_specs=in_specs,
        out_specs=[pl.BlockSpec((R, inner), lambda i: (i, 0)),
                   pl.BlockSpec((None, ns, inner, N), lambda i: (layer, i, 0, 0))],
        out_shape=[jax.ShapeDtypeStruct((M, inner), F32), jax.ShapeDtypeStruct((depth, nseq, inner, N), F32)],
        scratch_shapes=scratch(R),
        input_output_aliases=aliases,
        compiler_params=_cparams("parallel"),
        name="ssd_sample",
    )(*args)
    return y, st.reshape(depth, nseq, heads, P, N)


def _merge_kernel(g0_ref, g1_ref, g2_ref, ya_ref, yb_ref, yc_ref, w_ref, x_ref, o_ref, h_ref):
    @pl.when(pl.program_id(1) == 0)
    def _():
        merged = (jax.nn.sigmoid(g0_ref[...]) * ya_ref[...] + jax.nn.sigmoid(g1_ref[...]) * yb_ref[...]
                  + jax.nn.sigmoid(g2_ref[...]) * yc_ref[...])
        h_ref[...] = merged.astype(BF16)
    o_ref[...] = x_ref[...] + jnp.dot(h_ref[...], w_ref[...], preferred_element_type=F32)


def merge_out(proj, ya, yb, yc, w_out, x, cols, *, tm=256, tn=2048):
    M, D = x.shape
    tm = _tile(M, tm)
    tn = _tile(D, tn, LANES)
    g0 = cols["gate"] // D
    assert cols["gate"] % D == 0

    def gate(k):
        return pl.BlockSpec((tm, D), lambda i, j: (i, g0 + k))

    def row():
        return pl.BlockSpec((tm, D), lambda i, j: (i, 0))

    return pl.pallas_call(
        _merge_kernel,
        grid=(M // tm, D // tn),
        in_specs=[gate(0), gate(1), gate(2), row(), row(), row(),
                  pl.BlockSpec((D, tn), lambda i, j: (0, j)), pl.BlockSpec((tm, tn), lambda i, j: (i, j))],
        out_specs=pl.BlockSpec((tm, tn), lambda i, j: (i, j)),
        out_shape=jax.ShapeDtypeStruct((M, D), F32),
        scratch_shapes=[pltpu.VMEM((tm, D), BF16)],
        compiler_params=_cparams("parallel", "arbitrary"),
        name="merge_out",
    )(proj, proj, proj, ya, yb, yc, w_out, x)


def _xattn_kernel(q_ref, k_ref, v_ref, o_ref, *, heads, scale):
    d = q_ref.shape[1] // heads
    for h in range(heads):
        sl = slice(h * d, (h + 1) * d)
        q = q_ref[:, sl].astype(BF16)
        sc = lax.dot_general(q, k_ref[:, sl].astype(BF16), (((1,), (1,)), ((), ())),
                             preferred_element_type=F32) * scale
        p = jnp.exp(sc - jnp.max(sc, axis=-1, keepdims=True))
        p = p / jnp.sum(p, axis=-1, keepdims=True)
        o_ref[:, sl] = jnp.dot(p.astype(BF16), v_ref[:, sl].astype(BF16),
                               preferred_element_type=F32).astype(o_ref.dtype)


def _xattn_cached_kernel(q_ref, k_ref, v_ref, o_ref, *, heads, scale, seqs):
    rows = q_ref.shape[0] // seqs
    for s in range(seqs):
        rs = slice(s * rows, (s + 1) * rows)
        q = q_ref[rs, :].astype(BF16)
        sc = lax.dot_general(q, k_ref[s].astype(BF16), (((1,), (1,)), ((), ())), preferred_element_type=F32) * scale
        sc = jnp.where(_cols(sc.shape) % heads == _rows(sc.shape) % heads, sc, -jnp.inf)
        p = jnp.exp(sc - jnp.max(sc, axis=-1, keepdims=True))
        p = p / jnp.sum(p, axis=-1, keepdims=True)
        o_ref[rs, :] = jnp.dot(p.astype(BF16), v_ref[s].astype(BF16), preferred_element_type=F32)


def cross_attend_cached(q, mem_k, mem_v, layer, L, *, seqs=8):
    M, W = q.shape
    depth, nseq, mem_len, heads, d = mem_k.shape
    seqs = _tile(nseq, seqs, 1)
    rows = seqs * L * heads

    def mem():
        return pl.BlockSpec((None, seqs, mem_len * heads, d), lambda i: (layer, i, 0, 0))

    out = pl.pallas_call(
        functools.partial(_xattn_cached_kernel, heads=heads, scale=d ** -0.5, seqs=seqs),
        grid=(nseq // seqs,),
        in_specs=[pl.BlockSpec((rows, d), lambda i: (i, 0)), mem(), mem()],
        out_specs=pl.BlockSpec((rows, d), lambda i: (i, 0)),
        out_shape=jax.ShapeDtypeStruct((M * heads, d), F32),
        compiler_params=_cparams("parallel"),
        name="cross_attend_cached",
    )(q.reshape(M * heads, d), mem_k.reshape(depth, nseq, mem_len * heads, d),
      mem_v.reshape(depth, nseq, mem_len * heads, d))
    return out.reshape(M, W)


def cross_attend(q, k, v, k_col, v_col, nseq, L, mem_len, heads, *, tq=512):
    M, W = q.shape
    tq = _tile(L, tq)
    nq = L // tq
    return pl.pallas_call(
        functools.partial(_xattn_kernel, heads=heads, scale=(W // heads) ** -0.5),
        grid=(nseq, nq),
        in_specs=[pl.BlockSpec((tq, W), lambda b, i: (b * nq + i, 0)),
                  pl.BlockSpec((mem_len, W), lambda b, i: (b, k_col)),
                  pl.BlockSpec((mem_len, W), lambda b, i: (b, v_col))],
        out_specs=pl.BlockSpec((tq, W), lambda b, i: (b * nq + i, 0)),
        out_shape=jax.ShapeDtypeStruct((M, W), F32),
        compiler_params=_cparams("parallel", "parallel"),
        name="cross_attend",
    )(q, k, v)


def _ffn_act_kernel(g_ref, u_ref, prev_ref, cw_ref, cb_ref, o_ref, *, seg, carried):
    prev = prev_ref[...]
    if carried:
        prev = jnp.where(pl.program_id(2) == 0, 0.0, prev)
    gate = _causal_conv(g_ref[...], prev, cw_ref[...], cb_ref[...], seg, carried)
    o_ref[...] = (jax.nn.gelu(gate, approximate=True) * u_ref[...]).astype(o_ref.dtype)


def ffn_act(up, nseq, L, state_conv, conv_w, conv_b, *, rows=256, cb=1536):
    M = up.shape[0]
    F = conv_w.shape[1]
    cb = _tile(F, cb, LANES)
    ncb = F // cb
    if state_conv is None:
        R = _tile(L, rows)
        nch = L // R
        in_specs = [pl.BlockSpec((R, cb), lambda b, j, c: (b * nch + c, j)),
                    pl.BlockSpec((R, cb), lambda b, j, c: (b * nch + c, ncb + j)),
                    pl.BlockSpec((SUBLANES, cb),
                                 lambda b, j, c: (jnp.maximum((b * nch + c) * (R // SUBLANES) - 1, 0), j)),
                    pl.BlockSpec((conv_w.shape[0], cb), lambda b, j, c: (0, j)),
                    pl.BlockSpec((1, cb), lambda b, j, c: (0, j))]
        return pl.pallas_call(
            functools.partial(_ffn_act_kernel, seg=R, carried=True),
            grid=(nseq, ncb, nch),
            in_specs=in_specs,
            out_specs=pl.BlockSpec((R, cb), lambda b, j, c: (b * nch + c, j)),
            out_shape=jax.ShapeDtypeStruct((M, F), BF16),
            compiler_params=_cparams("parallel", "parallel", "arbitrary"),
            name="ffn_act_prompt",
        )(up, up, up, conv_w, conv_b)
    R = _tile(M, rows)
    prev = _pad_state_rows(state_conv)
    in_specs = [pl.BlockSpec((R, cb), lambda i, j: (i, j)),
                pl.BlockSpec((R, cb), lambda i, j: (i, ncb + j)),
                pl.BlockSpec((R, cb), lambda i, j: (i, j)),
                pl.BlockSpec((conv_w.shape[0], cb), lambda i, j: (0, j)),
                pl.BlockSpec((1, cb), lambda i, j: (0, j))]
    return pl.pallas_call(
        functools.partial(_ffn_act_kernel, seg=L, carried=False),
        grid=(M // R, ncb),
        in_specs=in_specs,
        out_specs=pl.BlockSpec((R, cb), lambda i, j: (i, j)),
        out_shape=jax.ShapeDtypeStruct((M, F), BF16),
        compiler_params=_cparams("parallel", "parallel"),
        name="ffn_act_sample",
    )(up, up, prev, conv_w, conv_b)


def _norm_kernel(x_ref, g_ref, o_ref):
    o_ref[...] = _rms(x_ref[...], g_ref[...])


def rms_norm_rows(x, g, *, tm=512):
    M, D = x.shape
    tm = _tile(M, tm)
    return pl.pallas_call(
        _norm_kernel,
        grid=(M // tm,),
        in_specs=[pl.BlockSpec((tm, D), lambda i: (i, 0)), pl.BlockSpec((1, D), lambda i: (0, 0))],
        out_specs=pl.BlockSpec((tm, D), lambda i: (i, 0)),
        out_shape=jax.ShapeDtypeStruct((M, D), F32),
        compiler_params=_cparams("parallel"),
        name="final_norm",
    )(x, g.reshape(1, D))


def _pad_cols(w, width):
    return jnp.pad(w, ((0, 0), (0, width - w.shape[1])))


def _rotate_half_cols(w):
    half = w.shape[-1] // 2
    return jnp.concatenate([-w[..., half:], w[..., :half]], axis=-1)


def _row(v, width=None):
    v = v.reshape(1, -1).astype(F32)
    return v if width is None else _pad_cols(v, width)


def _layer_weights(l, dims, w_in, p):
    D, W, qr, R, rope, inner, cdim, heads_s = (dims[k] for k in
                                                ("D", "lru", "q_rank", "kv_rank", "rope", "inner", "cdim", "ssd_heads"))
    splits = (W, qr, R, rope, inner, cdim, heads_s, 3 * D)
    offs = [0]
    for s in splits:
        offs.append(offs[-1] + s)
    w = w_in[l]
    w_lru, w_qc, w_ckv, w_kr, w_z, w_xbc, w_dt, w_gate = (w[:, offs[i]:offs[i + 1]] for i in range(8))
    small = jnp.concatenate([_pad_cols(w_kr, LANES), _pad_cols(_rotate_half_cols(w_kr), LANES),
                             _pad_cols(w_dt, 2 * LANES)], axis=1)
    assert small.shape[1] == PROJ_BLOCK
    w_main = jnp.concatenate([w_lru, w_z, w_xbc, w_gate, w_qc, w_ckv, small], axis=1).astype(BF16)
    cols, o = {}, 0
    for name, width in (("lru", W), ("z", inner), ("xbc", cdim), ("gate", 3 * D), ("qc", qr), ("ckv", R),
                        ("small", PROJ_BLOCK)):
        assert o % PROJ_BLOCK == 0
        cols[name] = o
        o += width

    H, nope = dims["heads"], dims["nope"]
    hw = nope + LANES
    wq = p["mla_w_qb"][l].reshape(qr, H, nope + rope)
    wq_n, wq_r = wq[..., :nope], wq[..., nope:]
    zpad = jnp.zeros((qr, H, LANES - rope), F32)
    wq_a = jnp.concatenate([wq_n, wq_r, zpad], axis=-1).reshape(qr, H * hw)
    wq_b = jnp.concatenate([_rotate_half_cols(wq_r), zpad], axis=-1).reshape(qr, H * LANES)
    w_uk = p["mla_w_uk"][l]
    w_uv = p["mla_w_uv"][l]
    wts = {
        "w_main": w_main,
        "norm_mix": p["norm_mix"][l],
        "lru_conv_w": p["lru_conv_w"][l], "lru_conv_b": _row(p["lru_conv_b"][l]),
        "lru_w_a": p["lru_w_a"][l].astype(BF16), "lru_w_x": p["lru_w_x"][l].astype(BF16),
        "lru_b_a": _row(p["lru_b_a"][l]), "lru_b_x": _row(p["lru_b_x"][l]), "lru_lambda": _row(p["lru_lambda"][l]),
        "mla_q_norm": p["mla_q_norm"][l], "mla_kv_norm": p["mla_kv_norm"][l],
        "wq": jnp.concatenate([wq_a, wq_b], axis=1).astype(BF16),
        "w_uk_flat": w_uk.reshape(R, H * nope).astype(BF16),
        "w_uv_flat": w_uv.reshape(R, -1).astype(BF16),
        "w_uk_heads": jnp.transpose(w_uk, (1, 2, 0)).astype(BF16),
        "w_uv_heads": jnp.transpose(w_uv, (1, 0, 2)).astype(BF16),
        "ssd_heads": heads_s,
        "ssd_conv_w": p["ssd_conv_w"][l], "ssd_conv_b": _row(p["ssd_conv_b"][l]),
        "ssd_dt_bias": _row(p["ssd_dt_bias"][l], LANES), "ssd_a_log": _row(p["ssd_a_log"][l], LANES),
        "ssd_d_x": _row(jnp.repeat(p["ssd_d"][l], SSD_HEAD_DIM)), "ssd_norm": _row(p["ssd_norm"][l]),
        "w_out": p["w_out"][l].astype(BF16),
        "norm_xa": p["norm_xa"][l], "norm_mem": p["norm_mem"][l],
        "xa_wq": p["xa_wq"][l].astype(BF16),
        "xa_wkv": jnp.concatenate([p["xa_wk"][l], p["xa_wv"][l]], axis=1).astype(BF16),
        "xa_wo": p["xa_wo"][l].astype(BF16),
        "norm_ffn": p["norm_ffn"][l],
        "ffn_w_up": p["ffn_w_up"][l].astype(BF16), "ffn_w_down": p["ffn_w_down"][l].astype(BF16),
        "ffn_conv_w": p["ffn_conv_w"][l], "ffn_conv_b": _row(p["ffn_conv_b"][l]),
    }
    return wts, cols


def _rope_tables(pos0, L, rope, nope, rows, qscale):
    inv = 1.0 / (ROPE_THETA ** (jnp.arange(0, rope, 2, dtype=F32) / rope))
    ang = (pos0 + jnp.arange(L)).astype(F32)[:, None] * inv[None, :]
    cos, sin = jnp.cos(ang), jnp.sin(ang)
    cos2, sin2 = jnp.concatenate([cos, cos], axis=1), jnp.concatenate([sin, sin], axis=1)
    zpad = jnp.zeros((L, LANES - rope), F32)
    ck, sk = jnp.concatenate([cos2, zpad], axis=1), jnp.concatenate([sin2, zpad], axis=1)
    cq = jnp.concatenate([jnp.ones((L, nope), F32), ck], axis=1) * qscale
    sq = jnp.concatenate([jnp.zeros((L, nope), F32), sk], axis=1) * qscale
    reps = max(rows // L, 1)
    return tuple(jnp.tile(t, (reps, 1)) for t in (cq, sq, ck, sk))


def _trunk_layer(x, nseq, L, wts, cols, dims, tables, state, mem, paged):
    D, W, R, rope, inner, cdim, F = (dims[k] for k in ("D", "lru", "kv_rank", "rope", "inner", "cdim", "d_ff"))
    H, nope = dims["heads"], dims["nope"]
    hw = nope + LANES
    prompt = state is None
    proj = mm(x, wts["w_main"], gain=wts["norm_mix"], name="in_proj")

    y_a = lru_mixer(proj, nseq, L, None if prompt else state["lru_h"], None if prompt else state["lru_conv"], wts)

    q2 = mm(proj, wts["wq"], gain=wts["mla_q_norm"], a_col=cols["qc"] // dims["q_rank"], name="q_proj")
    q_all, ckv, kr = mla_prep(q2, proj, tables, wts["mla_kv_norm"], cols)
    if prompt:
        kn = mm(ckv, wts["w_uk_flat"], out_dtype=BF16, name="k_up")
        v = mm(ckv, wts["w_uv_flat"], out_dtype=BF16, name="v_up")
        y_b = mha_prompt(q_all, kn, kr, v, nseq, L, H)
    else:
        q_lat = hmm(q_all, wts["w_uk_heads"], a_stride=hw // nope, out_dtype=BF16, name="q_absorb")
        o_lat = mqa_paged(q_lat, q_all, ckv, kr, paged["pool_c"], paged["pool_kt"], paged["layer"],
                          paged["page_table"], H)
        y_b = hmm(o_lat, wts["w_uv_heads"], name="v_up_heads")

    if prompt:
        y_c, ssd_h = ssd_mixer(proj, nseq, L, None, None, None, None, wts, cols)
    else:
        y_c, ssd_h = ssd_mixer(proj, nseq, L, state["ssd_h"], state["layer"], state["ssd_h_out"],
                               state["ssd_conv"], wts, cols)

    x = merge_out(proj, y_a, y_b, y_c, wts["w_out"], x, cols)

    q = mm(x, wts["xa_wq"], gain=wts["norm_xa"], name="xa_q")
    if prompt:
        o = cross_attend(q, mem, mem, 0, 1, nseq, L, dims["mem_len"], dims["xa_heads"])
    else:
        o = cross_attend_cached(q, mem[0], mem[1], mem[2], L)
    x = mm(o, wts["xa_wo"], res=x, name="xa_o")

    up = mm(x, wts["ffn_w_up"], gain=wts["norm_ffn"], name="ffn_up")
    act = ffn_act(up, nseq, L, None if prompt else state["ffn_conv"], wts["ffn_conv_w"], wts["ffn_conv_b"])
    x = mm(act, wts["ffn_w_down"], res=x, name="ffn_down")

    def tail(a, c0, width, n):
        return a.reshape(nseq, L, -1)[:, L - n:, c0:c0 + width]

    rows = (ckv.reshape(nseq, L, R), kr.reshape(nseq, L, LANES)[..., :rope],
            y_a.reshape(nseq, L, W)[:, L - 1], tail(proj, cols["lru"], W, 3),
            ssd_h, tail(proj, cols["xbc"], cdim, 3), tail(up, 0, F, 2))
    return x, rows


def kernel(x_prompt, x_sample, cache_ckv, cache_krope, cache_mem_k, cache_mem_v, state_lru_h, state_lru_conv, state_ssd_h, state_ssd_conv, state_ffn_conv, page_table, mem_prompt, norm_mix, w_in, lru_conv_w, lru_conv_b, lru_w_a, lru_b_a, lru_w_x, lru_b_x, lru_lambda, mla_q_norm, mla_w_qb, mla_kv_norm, mla_w_uk, mla_w_uv, ssd_conv_w, ssd_conv_b, ssd_dt_bias, ssd_a_log, ssd_d, ssd_norm, w_out, norm_xa, norm_mem, xa_wq, xa_wk, xa_wv, xa_wo, norm_ffn, ffn_w_up, ffn_conv_w, ffn_conv_b, ffn_w_down, norm_final):
    p = dict(norm_mix=norm_mix, lru_conv_w=lru_conv_w, lru_conv_b=lru_conv_b, lru_w_a=lru_w_a, lru_b_a=lru_b_a,
             lru_w_x=lru_w_x, lru_b_x=lru_b_x, lru_lambda=lru_lambda, mla_q_norm=mla_q_norm, mla_w_qb=mla_w_qb,
             mla_kv_norm=mla_kv_norm, mla_w_uk=mla_w_uk, mla_w_uv=mla_w_uv, ssd_conv_w=ssd_conv_w,
             ssd_conv_b=ssd_conv_b, ssd_dt_bias=ssd_dt_bias, ssd_a_log=ssd_a_log, ssd_d=ssd_d, ssd_norm=ssd_norm,
             w_out=w_out, norm_xa=norm_xa, norm_mem=norm_mem, xa_wq=xa_wq, xa_wk=xa_wk, xa_wv=xa_wv, xa_wo=xa_wo,
             norm_ffn=norm_ffn, ffn_w_up=ffn_w_up, ffn_conv_w=ffn_conv_w, ffn_conv_b=ffn_conv_b,
             ffn_w_down=ffn_w_down)
    depth = w_in.shape[0]
    bp, lp, D = x_prompt.shape
    bs, ls, _ = x_sample.shape
    n_pages = page_table.shape[1]
    page = cache_ckv.shape[2]
    mem_len, xa_heads, xa_hd = cache_mem_k.shape[2:]
    R, H, nope = mla_w_uk.shape[1:]
    dims = dict(D=D, lru=lru_conv_w.shape[-1], q_rank=mla_q_norm.shape[-1], kv_rank=R, rope=cache_krope.shape[-1],
                inner=ssd_norm.shape[-1], cdim=ssd_conv_w.shape[-1], ssd_heads=ssd_a_log.shape[-1],
                d_ff=ffn_conv_w.shape[-1], heads=H, nope=nope, mem_len=mem_len, xa_heads=xa_heads)
    xa_w = xa_heads * xa_hd
    qscale = float(nope + dims["rope"]) ** -0.5 * LOG2_E
    tab_p = _rope_tables(0, lp, dims["rope"], nope, lp, qscale)
    tab_s = _rope_tables(n_pages * page, ls, dims["rope"], nope, min(256, bs * ls), qscale)
    pool_kt = jnp.swapaxes(cache_krope, 2, 3)

    xp = x_prompt.reshape(bp * lp, D)
    xs = x_sample.reshape(bs * ls, D)
    mem_rows = mem_prompt.reshape(bp * mem_len, D)
    p_rows, s_rows, p_mem_k, p_mem_v = [], [], [], []
    s_ssd_h = None
    for l in range(depth):
        wts, cols = _layer_weights(l, dims, w_in, p)
        mkv = mm(mem_rows, wts["xa_wkv"], gain=wts["norm_mem"], name="mem_kv")
        p_mem_k.append(mkv[:, :xa_w].reshape(bp, mem_len, xa_heads, xa_hd))
        p_mem_v.append(mkv[:, xa_w:].reshape(bp, mem_len, xa_heads, xa_hd))
        xp, rows = _trunk_layer(xp, bp, lp, wts, cols, dims, tab_p, None, mkv, None)
        p_rows.append(rows)
        state = dict(lru_h=state_lru_h[l], lru_conv=state_lru_conv[l], ssd_h=state_ssd_h, layer=l,
                     ssd_h_out=s_ssd_h, ssd_conv=state_ssd_conv[l], ffn_conv=state_ffn_conv[l])
        paged = dict(pool_c=cache_ckv, pool_kt=pool_kt, layer=l, page_table=page_table)
        xs, rows = _trunk_layer(xs, bs, ls, wts, cols, dims, tab_s, state, (cache_mem_k, cache_mem_v, l), paged)
        s_ssd_h = rows[4]
        s_rows.append(rows)
    y_prompt = rms_norm_rows(xp, norm_final).reshape(bp, lp, D)
    y_sample = rms_norm_rows(xs, norm_final).reshape(bs, ls, D)
    p_out = [jnp.stack(r) for r in zip(*p_rows)]
    s_out = [s_ssd_h if i == 4 else jnp.stack(r) for i, r in enumerate(zip(*s_rows))]
    return (y_prompt, y_sample, *p_out, jnp.stack(p_mem_k), jnp.stack(p_mem_v), *s_out)
```

```python
import functools

import jax
import jax.numpy as jnp
from jax import lax
from jax.experimental import pallas as pl
from jax.experimental.pallas import tpu as pltpu

F32 = jnp.float32
BF16 = jnp.bfloat16
EPS = 1e-6
LRU_C = 8.0
ROPE_THETA = 10000.0
LANES = 128
SUBLANES = 8
VMEM_LIMIT_BYTES = 56 * 2**20
SSD_CHUNK = 128
SSD_STATE = 128
SSD_HEAD_DIM = 64
LRU_BLOCK = 128
PAGES_PER_STEP = 32
PAGES_PER_GROUP = 32
LOG2_E = 1.4426950408889634
PROJ_BLOCK = 512


def _cparams(*sem):
    return pltpu.CompilerParams(dimension_semantics=sem, vmem_limit_bytes=VMEM_LIMIT_BYTES)


def _tile(n, pref, mult=SUBLANES):
    if n <= pref:
        return n
    t = pref - pref % mult
    while t > mult and n % t:
        t -= mult
    assert n % t == 0, (n, pref, mult)
    return t


def _rows(shape):
    return lax.broadcasted_iota(jnp.int32, shape, 0)


def _cols(shape):
    return lax.broadcasted_iota(jnp.int32, shape, 1)


def _rms(x, g):
    return x * lax.rsqrt(jnp.mean(x * x, axis=-1, keepdims=True) + EPS) * g


def _sigmoid(x):
    return 0.5 * jnp.tanh(0.5 * x) + 0.5


def _silu(x):
    return x * _sigmoid(x)


def _bf16_terms(v):
    v1 = v.astype(BF16)
    r1 = v - v1.astype(F32)
    v2 = r1.astype(BF16)
    v3 = (r1 - v2.astype(F32)).astype(BF16)
    return jnp.concatenate([v1, v2, v3], axis=1)


def _mm_kernel(*refs, has_gain, has_res, stage):
    it = iter(refs)
    a_ref = next(it)
    g_ref = next(it) if has_gain else None
    w_ref = next(it)
    r_ref = next(it) if has_res else None
    o_ref = next(it)
    h_ref = next(it) if stage else None
    if stage:
        @pl.when(pl.program_id(1) == 0)
        def _():
            a = a_ref[...].astype(F32)
            if has_gain:
                a = _rms(a, g_ref[...])
            h_ref[...] = a.astype(BF16)
        h = h_ref[...]
    else:
        h = a_ref[...]
    acc = jnp.dot(h, w_ref[...], preferred_element_type=F32)
    if has_res:
        acc = acc + r_ref[...]
    o_ref[...] = acc.astype(o_ref.dtype)


def mm(a, w, *, gain=None, res=None, a_col=0, out_dtype=F32, tm=1024, tn=512, name="mm"):
    M = a.shape[0]
    K, N = w.shape
    tm = _tile(M, tm)
    tn = _tile(N, tn, LANES)
    has_gain, has_res = gain is not None, res is not None
    stage = has_gain or a.dtype != BF16
    in_specs = [pl.BlockSpec((tm, K), lambda i, j: (i, a_col))]
    args = [a]
    if has_gain:
        in_specs.append(pl.BlockSpec((1, K), lambda i, j: (0, 0)))
        args.append(gain.reshape(1, K).astype(F32))
    in_specs.append(pl.BlockSpec((K, tn), lambda i, j: (0, j)))
    args.append(w)
    if has_res:
        in_specs.append(pl.BlockSpec((tm, tn), lambda i, j: (i, j)))
        args.append(res)
    return pl.pallas_call(
        functools.partial(_mm_kernel, has_gain=has_gain, has_res=has_res, stage=stage),
        grid=(M // tm, N // tn),
        in_specs=in_specs,
        out_specs=pl.BlockSpec((tm, tn), lambda i, j: (i, j)),
        out_shape=jax.ShapeDtypeStruct((M, N), out_dtype),
        scratch_shapes=[pltpu.VMEM((tm, K), BF16)] if stage else [],
        compiler_params=_cparams("parallel", "arbitrary"),
        name=name,
    )(*args)


def _hmm_kernel(a_ref, w_ref, o_ref):
    o_ref[...] = jnp.dot(a_ref[...].astype(BF16), w_ref[...], preferred_element_type=F32).astype(o_ref.dtype)


def hmm(a, w, *, a_stride=1, out_dtype=F32, tm=1024, name="hmm"):
    M = a.shape[0]
    H, Ka, Nb = w.shape
    tm = _tile(M, tm)
    return pl.pallas_call(
        _hmm_kernel,
        grid=(M // tm, H),
        in_specs=[pl.BlockSpec((tm, Ka), lambda i, h: (i, h * a_stride)),
                  pl.BlockSpec((None, Ka, Nb), lambda i, h: (h, 0, 0))],
        out_specs=pl.BlockSpec((tm, Nb), lambda i, h: (i, h)),
        out_shape=jax.ShapeDtypeStruct((M, H * Nb), out_dtype),
        compiler_params=_cparams("parallel", "parallel"),
        name=name,
    )(a, w)


def _shift_rows(x, s, width, prev, seg, halo):
    R = x.shape[0]
    xr = pltpu.roll(x, s, 0)
    if not halo:
        up = width - 1 - s
        pr = pltpu.roll(prev, (R - up) % R, 0) if up else prev
        return jnp.where(_rows(x.shape) % seg >= s, xr, pr)
    pr = pltpu.roll(prev, s, 0)
    top = jnp.where(_rows(pr.shape) >= s, xr[:SUBLANES], pr)
    if R == SUBLANES:
        return top
    return jnp.concatenate([top, xr[SUBLANES:]], axis=0)


def _causal_conv(x, prev, w, b, seg, halo):
    width = w.shape[0]
    y = x * w[width - 1:width]
    for s in range(1, width):
        y = y + _shift_rows(x, s, width, prev, seg, halo) * w[width - 1 - s:width - s]
    return y + b


def _pad_state_rows(buf):
    nb, wm1, c = buf.shape
    return jnp.pad(buf, ((0, 0), (0, SUBLANES - wm1), (0, 0))).reshape(nb * SUBLANES, c)


def _segment_scan(a, u, seg):
    rid = _rows(a.shape) % seg
    d = 1
    while d < seg:
        ok = rid >= d
        a_sh = pltpu.roll(a, d, 0)
        u_sh = pltpu.roll(u, d, 0)
        u = jnp.where(ok, a * u_sh + u, u)
        a = jnp.where(ok, a * a_sh, a)
        d *= 2
    return a, u


def _segment_cumsum(x, seg):
    rid = _rows(x.shape) % seg
    d = 1
    while d < seg:
        x = jnp.where(rid >= d, x + pltpu.roll(x, d, 0), x)
        d *= 2
    return x


def _lru_kernel(*refs, seg, carried):
    if carried:
        x_ref, prev_ref, cw_ref, cb_ref, wa_ref, wx_ref, ba_ref, bx_ref, lam_ref, y_ref, carry_ref = refs
        h0_ref = None
        first = pl.program_id(2) == 0

        @pl.when(first)
        def _():
            carry_ref[...] = jnp.zeros_like(carry_ref)
    else:
        x_ref, prev_ref, cw_ref, cb_ref, wa_ref, wx_ref, ba_ref, bx_ref, lam_ref, h0_ref, y_ref = refs
    for k in range(x_ref.shape[1] // LRU_BLOCK):
        sl = slice(k * LRU_BLOCK, (k + 1) * LRU_BLOCK)
        x = x_ref[:, sl]
        prev = prev_ref[:, sl]
        if carried:
            prev = jnp.where(first, 0.0, prev)
        xc = _causal_conv(x, prev, cw_ref[:, sl], cb_ref[:, sl], seg, carried)
        xb = xc.astype(BF16)
        r = _sigmoid(jnp.dot(xb, wa_ref[k], preferred_element_type=F32) + ba_ref[:, sl])
        i = _sigmoid(jnp.dot(xb, wx_ref[k], preferred_element_type=F32) + bx_ref[:, sl])
        lam = lam_ref[:, sl]
        softplus_neg_lam = jnp.maximum(-lam, 0.0) + jnp.log1p(jnp.exp(-jnp.abs(lam)))
        log_a = -LRU_C * r * softplus_neg_lam
        a = jnp.exp(log_a)
        u = jnp.sqrt(-jnp.tanh(log_a) * (a * a + 1.0)) * (i * xc)
        if not carried:
            u = u + a * h0_ref[:, sl]
            y_ref[:, sl] = _segment_scan(a, u, seg)[1]
        else:
            a8, u8 = _segment_scan(a, u, SUBLANES)
            hprev = carry_ref[:, sl]
            for g in range(seg // SUBLANES):
                gs = slice(g * SUBLANES, (g + 1) * SUBLANES)
                hg = u8[gs] + a8[gs] * hprev
                y_ref[gs, sl] = hg
                hprev = hg[SUBLANES - 1:SUBLANES]
            carry_ref[:, sl] = hprev


def lru_mixer(proj, nseq, L, state_h, state_conv, wts, *, cb=512, rows=256):
    M = proj.shape[0]
    W = wts["lru_conv_w"].shape[1]
    cb = _tile(W, cb, LANES)
    nb = cb // LRU_BLOCK
    common = [wts["lru_conv_w"], wts["lru_conv_b"], wts["lru_w_a"], wts["lru_w_x"], wts["lru_b_a"],
              wts["lru_b_x"], wts["lru_lambda"]]
    if state_h is None:
        R = _tile(L, rows)
        nch = L // R

        def vec(n):
            return pl.BlockSpec((n, cb), lambda b, j, c: (0, j))

        def gates():
            return pl.BlockSpec((nb, LRU_BLOCK, LRU_BLOCK), lambda b, j, c: (j, 0, 0))

        in_specs = [
            pl.BlockSpec((R, cb), lambda b, j, c: (b * nch + c, j)),
            pl.BlockSpec((SUBLANES, cb), lambda b, j, c: (jnp.maximum((b * nch + c) * (R // SUBLANES) - 1, 0), j)),
            vec(4), vec(1), gates(), gates(), vec(1), vec(1), vec(1),
        ]
        return pl.pallas_call(
            functools.partial(_lru_kernel, seg=R, carried=True),
            grid=(nseq, W // cb, nch),
            in_specs=in_specs,
            out_specs=pl.BlockSpec((R, cb), lambda b, j, c: (b * nch + c, j)),
            out_shape=jax.ShapeDtypeStruct((M, W), F32),
            scratch_shapes=[pltpu.VMEM((1, cb), F32)],
            compiler_params=_cparams("parallel", "parallel", "arbitrary"),
            name="lru_prompt",
        )(proj, proj, *common)
    R = _tile(M, rows)
    h0_rows = jnp.pad(state_h[:, None, :], ((0, 0), (0, L - 1), (0, 0))).reshape(M, W)
    prev = _pad_state_rows(state_conv)

    def vec(n):
        return pl.BlockSpec((n, cb), lambda i, j: (0, j))

    def tile():
        return pl.BlockSpec((R, cb), lambda i, j: (i, j))

    def gates():
        return pl.BlockSpec((nb, LRU_BLOCK, LRU_BLOCK), lambda i, j: (j, 0, 0))

    in_specs = [tile(), tile(), vec(4), vec(1), gates(), gates(), vec(1), vec(1), vec(1), tile()]
    return pl.pallas_call(
        functools.partial(_lru_kernel, seg=L, carried=False),
        grid=(M // R, W // cb),
        in_specs=in_specs,
        out_specs=tile(),
        out_shape=jax.ShapeDtypeStruct((M, W), F32),
        compiler_params=_cparams("parallel", "parallel"),
        name="lru_sample",
    )(proj, prev, *common, h0_rows)


def _mla_prep_kernel(qa_ref, qb_ref, ckv_ref, kr_ref, cq_ref, sq_ref, ck_ref, sk_ref, g_ref,
                     q_ref, ckvn_ref, krr_ref):
    hw = cq_ref.shape[1]
    nope = hw - LANES
    cq, sq = cq_ref[...], sq_ref[:, nope:]
    for h in range(q_ref.shape[1] // hw):
        sl = slice(h * hw, (h + 1) * hw)
        qa = qa_ref[:, sl] * cq
        rot = qb_ref[:, h * LANES:(h + 1) * LANES] * sq
        q_ref[:, sl] = jnp.concatenate([qa[:, :nope], qa[:, nope:] + rot], axis=1).astype(q_ref.dtype)
    ckvn_ref[...] = _rms(ckv_ref[...], g_ref[...])
    kr = kr_ref[...]
    krr_ref[...] = kr[:, :LANES] * ck_ref[...] + kr[:, LANES:2 * LANES] * sk_ref[...]


def mla_prep(q2, proj, tables, kv_gain, cols, *, tm=256):
    M = q2.shape[0]
    cq, sq, ck, sk = tables
    hw = cq.shape[1]
    heads = q2.shape[1] // (hw + LANES)
    qw, qbw = heads * hw, heads * LANES
    assert qw % qbw == 0
    tm = _tile(min(M, cq.shape[0]), tm)
    nt = cq.shape[0] // tm
    R = kv_gain.shape[0]

    def tab(w):
        return pl.BlockSpec((tm, w), lambda i: (i % nt, 0))

    return pl.pallas_call(
        _mla_prep_kernel,
        grid=(M // tm,),
        in_specs=[pl.BlockSpec((tm, qw), lambda i: (i, 0)), pl.BlockSpec((tm, qbw), lambda i: (i, qw // qbw)),
                  pl.BlockSpec((tm, R), lambda i: (i, cols["ckv"] // R)),
                  pl.BlockSpec((tm, PROJ_BLOCK), lambda i: (i, cols["small"] // PROJ_BLOCK)),
                  tab(cq.shape[1]), tab(cq.shape[1]), tab(LANES), tab(LANES),
                  pl.BlockSpec((1, R), lambda i: (0, 0))],
        out_specs=[pl.BlockSpec((tm, qw), lambda i: (i, 0)), pl.BlockSpec((tm, R), lambda i: (i, 0)),
                   pl.BlockSpec((tm, LANES), lambda i: (i, 0))],
        out_shape=[jax.ShapeDtypeStruct((M, qw), BF16), jax.ShapeDtypeStruct((M, R), F32),
                   jax.ShapeDtypeStruct((M, LANES), F32)],
        compiler_params=_cparams("parallel"),
        name="mla_prep",
    )(q2, q2, proj, proj, cq, sq, ck, sk, kv_gain.reshape(1, R))


def _mha_kernel(q_ref, kn_ref, kr_ref, v_ref, o_ref, kcat_ref, vaug_ref, *, tk, nq):
    qi = pl.program_id(2)
    dk = kn_ref.shape[1]
    dv = v_ref.shape[1]

    @pl.when(qi == 0)
    def _():
        kcat_ref[:, :dk] = kn_ref[...]
        kcat_ref[:, dk:] = kr_ref[...].astype(BF16)
        vaug_ref[:, :dv] = v_ref[...]
        vaug_ref[:, dv:] = jnp.ones((vaug_ref.shape[0], vaug_ref.shape[1] - dv), BF16)

    q = q_ref[...]
    dn = (((1,), (1,)), ((), ()))
    for n in range(nq):
        @pl.when(qi == n)
        def _(n=n):
            lo, hi = n * tk, (n + 1) * tk
            sd = lax.dot_general(q, kcat_ref[lo:hi, :], dn, preferred_element_type=F32)
            sd = jnp.where(_cols(sd.shape) <= _rows(sd.shape), sd, -jnp.inf)
            m = jnp.max(sd, axis=-1, keepdims=True)
            if n:
                sp = lax.dot_general(q, kcat_ref[0:lo, :], dn, preferred_element_type=F32)
                m = jnp.maximum(m, jnp.max(sp, axis=-1, keepdims=True))
            acc = jnp.dot(jnp.exp2(sd - m).astype(BF16), vaug_ref[lo:hi, :], preferred_element_type=F32)
            if n:
                acc = acc + jnp.dot(jnp.exp2(sp - m).astype(BF16), vaug_ref[0:lo, :], preferred_element_type=F32)
            o_ref[...] = (acc[:, :dv] / acc[:, dv:2 * dv]).astype(o_ref.dtype)


def mha_prompt(q, kn, kr, v, nseq, L, heads, *, tq=512):
    M = q.shape[0]
    hw = q.shape[1] // heads
    dv = v.shape[1] // heads
    tq = _tile(L, tq)
    nq = L // tq
    return pl.pallas_call(
        functools.partial(_mha_kernel, tk=tq, nq=nq),
        grid=(nseq, heads, nq),
        in_specs=[pl.BlockSpec((tq, hw), lambda b, h, i: (b * nq + i, h)),
                  pl.BlockSpec((L, hw - LANES), lambda b, h, i: (b, h)),
                  pl.BlockSpec((L, LANES), lambda b, h, i: (b, 0)),
                  pl.BlockSpec((L, dv), lambda b, h, i: (b, h))],
        out_specs=pl.BlockSpec((tq, dv), lambda b, h, i: (b * nq + i, h)),
        out_shape=jax.ShapeDtypeStruct((M, heads * dv), F32),
        scratch_shapes=[pltpu.VMEM((L, hw), BF16), pltpu.VMEM((L, 2 * dv), BF16)],
        compiler_params=_cparams("parallel", "parallel", "arbitrary"),
        name="mha_prompt",
    )(q, kn, kr, v)


def _mqa_paged_kernel(pt_ref, ql_ref, qr_ref, *refs, pps, group, rope, heads):
    del pt_ref
    c_refs = refs[:pps]
    k_refs = refs[pps:2 * pps]
    cn_ref, kn_ref, o_ref, m_ref, l_ref, acc_ref, cbuf_ref, kbuf_ref = refs[2 * pps:]
    step = pl.program_id(1)
    page = c_refs[0].shape[0]

    @pl.when(step == 0)
    def _():
        m_ref[...] = jnp.full_like(m_ref, -jnp.inf)
        l_ref[...] = jnp.zeros_like(l_ref)
        acc_ref[...] = jnp.zeros_like(acc_ref)

    ql = ql_ref[...]
    qr = qr_ref[:, LANES:LANES + rope]
    dn = (((1,), (1,)), ((), ()))

    def update(s, c):
        m = m_ref[...]
        m_new = jnp.maximum(m, jnp.max(s, axis=-1, keepdims=True))
        alpha = jnp.exp2(m - m_new)
        p = jnp.exp2(s - m_new)
        m_ref[...] = m_new
        l_ref[...] = alpha * l_ref[...] + jnp.sum(p, axis=-1, keepdims=True)
        acc_ref[...] = alpha * acc_ref[...] + jnp.dot(p.astype(BF16), c, preferred_element_type=F32)

    for k, (c_ref, k_ref) in enumerate(zip(c_refs, k_refs)):
        cbuf_ref[k * page:(k + 1) * page, :] = c_ref[...].astype(BF16)
        kbuf_ref[:, k * page:(k + 1) * page] = k_ref[...].astype(BF16)
    for g0 in range(0, pps, group):
        c = cbuf_ref[g0 * page:(g0 + group) * page, :]
        s = lax.dot_general(ql, c, dn, preferred_element_type=F32)
        s = s + jnp.dot(qr, kbuf_ref[:, g0 * page:(g0 + group) * page], preferred_element_type=F32)
        update(s, c)

    @pl.when(step == pl.num_programs(1) - 1)
    def _():
        c = cn_ref[...].astype(BF16)
        s = lax.dot_general(ql, c, dn, preferred_element_type=F32)
        s = s + lax.dot_general(qr, kn_ref[:, :rope].astype(BF16), dn, preferred_element_type=F32)
        s = jnp.where(_cols(s.shape) <= _rows(s.shape) // heads, s, -jnp.inf)
        update(s, c)
        o_ref[...] = (acc_ref[...] / l_ref[...]).astype(o_ref.dtype)


def mqa_paged(q_lat, q_all, ckv_new, kr_new, pool_c, pool_kt, layer, page_table, heads):
    nb, n_pages = page_table.shape
    T = ckv_new.shape[0] // nb
    R = pool_c.shape[-1]
    rope = pool_kt.shape[2]
    page = pool_c.shape[2]
    pps = _tile(n_pages, PAGES_PER_STEP, 1)
    group = _tile(pps, PAGES_PER_GROUP, 1)
    rows = T * heads
    ql = q_lat.reshape(nb * rows, R)
    qa = q_all.reshape(nb * rows, q_all.shape[1] // heads)

    def page_spec(shape, k):
        return pl.BlockSpec((None, None) + shape, lambda b, s, pt: (layer, pt[b, s * pps + k], 0, 0))

    grid_spec = pltpu.PrefetchScalarGridSpec(
        num_scalar_prefetch=1,
        grid=(nb, n_pages // pps),
        in_specs=[pl.BlockSpec((rows, R), lambda b, s, pt: (b, 0)),
                  pl.BlockSpec((rows, qa.shape[1]), lambda b, s, pt: (b, 0))]
                 + [page_spec((page, R), k) for k in range(pps)]
                 + [page_spec((rope, page), k) for k in range(pps)]
                 + [pl.BlockSpec((T, R), lambda b, s, pt: (b, 0)),
                    pl.BlockSpec((T, LANES), lambda b, s, pt: (b, 0))],
        out_specs=pl.BlockSpec((rows, R), lambda b, s, pt: (b, 0)),
        scratch_shapes=[pltpu.VMEM((rows, 1), F32), pltpu.VMEM((rows, 1), F32), pltpu.VMEM((rows, R), F32),
                        pltpu.VMEM((pps * page, R), BF16), pltpu.VMEM((rope, pps * page), BF16)],
    )
    out = pl.pallas_call(
        functools.partial(_mqa_paged_kernel, pps=pps, group=group, rope=rope, heads=heads),
        grid_spec=grid_spec,
        out_shape=jax.ShapeDtypeStruct((nb * rows, R), BF16),
        compiler_params=_cparams("parallel", "arbitrary"),
        name="mqa_paged",
    )(page_table, ql, qa, *([pool_c] * pps), *([pool_kt] * pps), ckv_new, kr_new)
    return out.reshape(nb * T, heads * R)


def _ssd_kernel(*refs, seg, carried, aliased, heads, conv_cb):
    if carried:
        (z_ref, xbc_ref, prev_ref, dt_ref, cw_ref, cb_ref, dtb_ref, alog_ref, dskx_ref, ng_ref, e_ref,
         y_ref, hout_ref, xc_ref, xd_ref, xdw_ref, yacc_ref, yoff_ref, al_ref, st_ref) = refs
        h0_ref = None
        first = pl.program_id(1) == 0

        @pl.when(first)
        def _():
            st_ref[...] = jnp.zeros_like(st_ref)
    else:
        refs = refs[:12] + refs[12 + int(aliased):]
        (z_ref, xbc_ref, prev_ref, dt_ref, cw_ref, cb_ref, dtb_ref, alog_ref, dskx_ref, ng_ref, e_ref, h0_ref,
         y_ref, hout_ref, xc_ref, xd_ref, xdw_ref, yacc_ref, yoff_ref, al_ref) = refs
    R = z_ref.shape[0]
    inner = z_ref.shape[1]
    P = inner // heads
    N = SSD_STATE
    G = (xbc_ref.shape[1] - inner) // (2 * N)
    hpg = heads // G
    gw = hpg * P
    dn_t = (((1,), (1,)), ((), ()))
    exact = lax.Precision.HIGHEST

    for k in range(xbc_ref.shape[1] // conv_cb):
        sl = slice(k * conv_cb, (k + 1) * conv_cb)
        prev = prev_ref[:, sl]
        if carried:
            prev = jnp.where(first, 0.0, prev)
        xc_ref[:, sl] = _silu(_causal_conv(xbc_ref[:, sl], prev, cw_ref[:, sl], cb_ref[:, sl], seg, carried))

    dt = jax.nn.softplus(dt_ref[...] + dtb_ref[...])
    acum = _segment_cumsum(dt * -jnp.exp(alog_ref[...]), seg)
    rid, cid = _rows((R, R)), _cols((R, R))
    seg_last = (cid == (rid // seg) * seg + (seg - 1)).astype(F32)
    alast = jnp.dot(seg_last, acum, precision=exact, preferred_element_type=F32)
    al_ref[...] = alast
    e_mat = e_ref[...]

    def per_channel(v):
        return jnp.dot(_bf16_terms(v), e_mat, preferred_element_type=F32)

    xd = xc_ref[:, :inner] * per_channel(dt)
    xd_ref[...] = xd
    xdw_ref[...] = xd * per_channel(jnp.exp(alast - acum))

    if R < LANES:
        acum_sq = jnp.concatenate([acum, jnp.zeros((LANES - R, LANES), F32)], axis=0)
    else:
        acum_sq = acum
    acum_t = acum_sq.T
    allowed = (cid <= rid) & (cid // seg == rid // seg)

    for g in range(G):
        bm = xc_ref[:, inner + g * N: inner + (g + 1) * N].astype(BF16)
        cm = xc_ref[:, inner + (G + g) * N: inner + (G + g + 1) * N].astype(BF16)
        cb_mat = lax.dot_general(cm, bm, dn_t, preferred_element_type=F32)
        for e in range(hpg):
            h = g * hpg + e
            decay = jnp.exp(jnp.where(allowed, acum[:, h:h + 1] - acum_t[h:h + 1, :R], -jnp.inf))
            yacc_ref[:, h * P:(h + 1) * P] = jnp.dot((cb_mat * decay).astype(BF16),
                                                     xd_ref[:, h * P:(h + 1) * P].astype(BF16),
                                                     preferred_element_type=F32)

    def per_segment(s, carry):
        rows = pl.ds(pl.multiple_of(s * seg, seg), seg)
        grow = jnp.exp(al_ref[pl.ds(pl.multiple_of(s * seg, seg), 1), :])
        for g in range(G):
            gs = slice(g * gw, (g + 1) * gw)
            bm = xc_ref[rows, inner + g * N: inner + (g + 1) * N].astype(BF16)
            cm = xc_ref[rows, inner + (G + g) * N: inner + (G + g + 1) * N].astype(BF16)
            hprev = st_ref[gs, :] if carried else h0_ref[s, gs, :]
            yoff_ref[rows, gs] = lax.dot_general(cm, hprev.astype(BF16), dn_t, preferred_element_type=F32)
            st = lax.dot_general(xdw_ref[rows, gs].astype(BF16), bm, (((0,), (0,)), ((), ())),
                                 preferred_element_type=F32)
            for e in range(hpg):
                h = g * hpg + e
                hnew = hprev[e * P:(e + 1) * P] * grow[:, h:h + 1] + st[e * P:(e + 1) * P]
                if carried:
                    st_ref[h * P:(h + 1) * P, :] = hnew
                    hout_ref[h * P:(h + 1) * P, :] = hnew
                else:
                    hout_ref[s, h * P:(h + 1) * P, :] = hnew
        return carry

    if R == seg:
        per_segment(0, 0)
    else:
        lax.fori_loop(0, R // seg, per_segment, 0)

    y = yacc_ref[...] + yoff_ref[...] * per_channel(jnp.exp(acum)) + dskx_ref[...] * xc_ref[:, :inner]
    y_ref[...] = _rms(y * _silu(z_ref[...]), ng_ref[...])


def ssd_mixer(proj, nseq, L, state_h, layer, prev_out, state_conv, wts, cols, *, seqs_per_tile=8):
    M = proj.shape[0]
    heads = wts["ssd_heads"]
    P, N = SSD_HEAD_DIM, SSD_STATE
    inner = heads * P
    cdim = wts["ssd_conv_w"].shape[1]
    conv_cb = PROJ_BLOCK
    zc, xc, dc = cols["z"] // inner, cols["xbc"] // cdim, (cols["small"] + 2 * LANES) // LANES
    assert cols["z"] % inner == 0 and cols["xbc"] % cdim == 0
    e_mat = (jnp.arange(3 * LANES)[:, None] % LANES == jnp.arange(inner)[None, :] // P).astype(BF16)
    vecs = [wts["ssd_conv_w"], wts["ssd_conv_b"], wts["ssd_dt_bias"], wts["ssd_a_log"], wts["ssd_d_x"],
            wts["ssd_norm"], e_mat]
    widths = [(4, cdim), (1, cdim), (1, LANES), (1, LANES), (1, inner), (1, inner), (3 * LANES, inner)]

    def scratch(R):
        return [pltpu.VMEM((R, cdim), F32), pltpu.VMEM((R, inner), F32), pltpu.VMEM((R, inner), F32),
                pltpu.VMEM((R, inner), F32), pltpu.VMEM((R, inner), F32), pltpu.VMEM((R, LANES), F32)]

    if state_h is None:
        Q = SSD_CHUNK if L % SSD_CHUNK == 0 else L
        nch = L // Q
        in_specs = [pl.BlockSpec((Q, inner), lambda b, c: (b * nch + c, zc)),
                    pl.BlockSpec((Q, cdim), lambda b, c: (b * nch + c, xc)),
                    pl.BlockSpec((SUBLANES, cdim),
                                 lambda b, c: (jnp.maximum((b * nch + c) * (Q // SUBLANES) - 1, 0), xc)),
                    pl.BlockSpec((Q, LANES), lambda b, c: (b * nch + c, dc))]
        in_specs += [pl.BlockSpec(w, lambda b, c: (0, 0)) for w in widths]
        y, st = pl.pallas_call(
            functools.partial(_ssd_kernel, seg=Q, carried=True, aliased=False, heads=heads, conv_cb=conv_cb),
            grid=(nseq, nch),
            in_specs=in_specs,
            out_specs=[pl.BlockSpec((Q, inner), lambda b, c: (b * nch + c, 0)),
                       pl.BlockSpec((None, inner, N), lambda b, c: (b, 0, 0))],
            out_shape=[jax.ShapeDtypeStruct((M, inner), F32), jax.ShapeDtypeStruct((nseq, inner, N), F32)],
            scratch_shapes=scratch(Q) + [pltpu.VMEM((inner, N), F32)],
            compiler_params=_cparams("parallel", "arbitrary"),
            name="ssd_prompt",
        )(proj, proj, proj, proj, *vecs)
        return y, st.reshape(nseq, heads, P, N)
    depth = state_h.shape[0]
    ns = _tile(nseq, seqs_per_tile, 1)
    R = ns * L
    prev = _pad_state_rows(state_conv)
    in_specs = [pl.BlockSpec((R, inner), lambda i: (i, zc)),
                pl.BlockSpec((R, cdim), lambda i: (i, xc)),
                pl.BlockSpec((R, cdim), lambda i: (i, 0)),
                pl.BlockSpec((R, LANES), lambda i: (i, dc))]
    in_specs += [pl.BlockSpec(w, lambda i: (0, 0)) for w in widths]
    in_specs += [pl.BlockSpec((None, ns, inner, N), lambda i: (layer, i, 0, 0))]
    args = [proj, proj, prev, proj, *vecs, state_h.reshape(depth, nseq, inner, N)]
    aliases = {}
    if prev_out is not None:
        in_specs.append(pl.BlockSpec(memory_space=pl.ANY))
        args.append(prev_out.reshape(depth, nseq, inner, N))
        aliases = {len(args) - 1: 1}
    y, st = pl.pallas_call(
        functools.partial(_ssd_kernel, seg=L, carried=False, aliased=prev_out is not None, heads=heads,
                          conv_cb=conv_cb),
        grid=(M // R,),
        in_specs=in_specs,
        out_specs=[pl.BlockSpec((R, inner), lambda i: (i, 0)),
                   pl.BlockSpec((None, ns, inner, N), lambda i: (layer, i, 0, 0))],
        out_shape=[jax.ShapeDtypeStruct((M, inner), F32), jax.ShapeDtypeStruct((depth, nseq, inner, N), F32)],
        scratch_shapes=scratch(R),
        input_output_aliases=aliases,
        compiler_params=_cparams("parallel"),
        name="ssd_sample",
    )(*args)
    return y, st.reshape(depth, nseq, heads, P, N)


def _merge_kernel(g0_ref, g1_ref, g2_ref, ya_ref, yb_ref, yc_ref, w_ref, x_ref, o_ref, h_ref):
    @pl.when(pl.program_id(1) == 0)
    def _():
        merged = (_sigmoid(g0_ref[...]) * ya_ref[...] + _sigmoid(g1_ref[...]) * yb_ref[...]
                  + _sigmoid(g2_ref[...]) * yc_ref[...])
        h_ref[...] = merged.astype(BF16)
    o_ref[...] = x_ref[...] + jnp.dot(h_ref[...], w_ref[...], preferred_element_type=F32)


def merge_out(proj, ya, yb, yc, w_out, x, cols, *, tm=256, tn=2048):
    M, D = x.shape
    tm = _tile(M, tm)
    tn = _tile(D, tn, LANES)
    g0 = cols["gate"] // D
    assert cols["gate"] % D == 0

    def gate(k):
        return pl.BlockSpec((tm, D), lambda i, j: (i, g0 + k))

    def row():
        return pl.BlockSpec((tm, D), lambda i, j: (i, 0))

    return pl.pallas_call(
        _merge_kernel,
        grid=(M // tm, D // tn),
        in_specs=[gate(0), gate(1), gate(2), row(), row(), row(),
                  pl.BlockSpec((D, tn), lambda i, j: (0, j)), pl.BlockSpec((tm, tn), lambda i, j: (i, j))],
        out_specs=pl.BlockSpec((tm, tn), lambda i, j: (i, j)),
        out_shape=jax.ShapeDtypeStruct((M, D), F32),
        scratch_shapes=[pltpu.VMEM((tm, D), BF16)],
        compiler_params=_cparams("parallel", "arbitrary"),
        name="merge_out",
    )(proj, proj, proj, ya, yb, yc, w_out, x)


def _xattn_kernel(q_ref, k_ref, v_ref, o_ref, *, heads, scale):
    d = q_ref.shape[1] // heads
    for h in range(heads):
        sl = slice(h * d, (h + 1) * d)
        q = q_ref[:, sl].astype(BF16)
        sc = lax.dot_general(q, k_ref[:, sl].astype(BF16), (((1,), (1,)), ((), ())),
                             preferred_element_type=F32) * scale
        p = jnp.exp(sc - jnp.max(sc, axis=-1, keepdims=True))
        p = p / jnp.sum(p, axis=-1, keepdims=True)
        o_ref[:, sl] = jnp.dot(p.astype(BF16), v_ref[:, sl].astype(BF16),
                               preferred_element_type=F32).astype(o_ref.dtype)


def _xattn_cached_kernel(q_ref, k_ref, v_ref, o_ref, *, heads, scale, seqs):
    rows = q_ref.shape[0] // seqs
    for s in range(seqs):
        rs = slice(s * rows, (s + 1) * rows)
        q = q_ref[rs, :].astype(BF16)
        sc = lax.dot_general(q, k_ref[s].astype(BF16), (((1,), (1,)), ((), ())), preferred_element_type=F32) * scale
        sc = jnp.where(_cols(sc.shape) % heads == _rows(sc.shape) % heads, sc, -jnp.inf)
        p = jnp.exp(sc - jnp.max(sc, axis=-1, keepdims=True))
        p = p / jnp.sum(p, axis=-1, keepdims=True)
        o_ref[rs, :] = jnp.dot(p.astype(BF16), v_ref[s].astype(BF16), preferred_element_type=F32)


def cross_attend_cached(q, mem_k, mem_v, layer, L, *, seqs=8):
    M, W = q.shape
    depth, nseq, mem_len, heads, d = mem_k.shape
    seqs = _tile(nseq, seqs, 1)
    rows = seqs * L * heads

    def mem():
        return pl.BlockSpec((None, seqs, mem_len * heads, d), lambda i: (layer, i, 0, 0))

    out = pl.pallas_call(
        functools.partial(_xattn_cached_kernel, heads=heads, scale=d ** -0.5, seqs=seqs),
        grid=(nseq // seqs,),
        in_specs=[pl.BlockSpec((rows, d), lambda i: (i, 0)), mem(), mem()],
        out_specs=pl.BlockSpec((rows, d), lambda i: (i, 0)),
        out_shape=jax.ShapeDtypeStruct((M * heads, d), F32),
        compiler_params=_cparams("parallel"),
        name="cross_attend_cached",
    )(q.reshape(M * heads, d), mem_k.reshape(depth, nseq, mem_len * heads, d),
      mem_v.reshape(depth, nseq, mem_len * heads, d))
    return out.reshape(M, W)


def cross_attend(q, k, v, k_col, v_col, nseq, L, mem_len, heads, *, tq=512):
    M, W = q.shape
    tq = _tile(L, tq)
    nq = L // tq
    return pl.pallas_call(
        functools.partial(_xattn_kernel, heads=heads, scale=(W // heads) ** -0.5),
        grid=(nseq, nq),
        in_specs=[pl.BlockSpec((tq, W), lambda b, i: (b * nq + i, 0)),
                  pl.BlockSpec((mem_len, W), lambda b, i: (b, k_col)),
                  pl.BlockSpec((mem_len, W), lambda b, i: (b, v_col))],
        out_specs=pl.BlockSpec((tq, W), lambda b, i: (b * nq + i, 0)),
        out_shape=jax.ShapeDtypeStruct((M, W), F32),
        compiler_params=_cparams("parallel", "parallel"),
        name="cross_attend",
    )(q, k, v)


def _ffn_up_act_kernel(*refs, seg, carried, tiles_per_seq):
    if carried:
        x_ref, xh_ref, g_ref, wg_ref, wu_ref, cw_ref, cb_ref, act_ref, gate_ref, h_ref, hh_ref = refs
    else:
        x_ref, g_ref, wg_ref, wu_ref, cw_ref, cb_ref, prev_ref, act_ref, gate_ref, h_ref = refs

    @pl.when(pl.program_id(1) == 0)
    def _():
        h_ref[...] = _rms(x_ref[...], g_ref[...]).astype(BF16)
        if carried:
            hh_ref[...] = _rms(xh_ref[...], g_ref[...]).astype(BF16)

    h = h_ref[...]
    wg = wg_ref[...]
    gate = jnp.dot(h, wg, preferred_element_type=F32)
    up = jnp.dot(h, wu_ref[...], preferred_element_type=F32)
    if carried:
        prev = jnp.dot(hh_ref[...], wg, preferred_element_type=F32)
        prev = jnp.where(pl.program_id(0) % tiles_per_seq == 0, 0.0, prev)
        gate_ref[...] = gate[gate.shape[0] - SUBLANES:]
    else:
        prev = prev_ref[...]
        gate_ref[...] = gate
    conv = _causal_conv(gate, prev, cw_ref[...], cb_ref[...], seg, carried)
    act_ref[...] = (jax.nn.gelu(conv, approximate=True) * up).astype(act_ref.dtype)


def ffn_up_act(x, gain, w_up, nseq, L, state_conv, conv_w, conv_b, *, tm=1024, tn=512):
    M, K = x.shape
    F = conv_w.shape[1]
    width = conv_w.shape[0]
    tn = _tile(F, tn, LANES)
    ncb = F // tn
    prompt = state_conv is None
    tm = _tile(L, tm) if prompt else _tile(M, tm)
    tps = L // tm if prompt else 1

    def wspec(off):
        return pl.BlockSpec((K, tn), lambda i, j: (0, off + j))

    def cspec(n):
        return pl.BlockSpec((n, tn), lambda i, j: (0, j))

    in_specs = [pl.BlockSpec((tm, K), lambda i, j: (i, 0))]
    args = [x]
    if prompt:
        in_specs.append(pl.BlockSpec((SUBLANES, K), lambda i, j: (jnp.maximum(i * (tm // SUBLANES) - 1, 0), 0)))
        args.append(x)
    in_specs += [pl.BlockSpec((1, K), lambda i, j: (0, 0)), wspec(0), wspec(ncb), cspec(width), cspec(1)]
    args += [gain.reshape(1, K), w_up, w_up, conv_w, conv_b]
    if prompt:
        gate_rows, gate_blk = (M // tm) * SUBLANES, SUBLANES
    else:
        in_specs.append(pl.BlockSpec((tm, tn), lambda i, j: (i, j)))
        args.append(_pad_state_rows(state_conv))
        gate_rows, gate_blk = M, tm
    act, gate = pl.pallas_call(
        functools.partial(_ffn_up_act_kernel, seg=tm if prompt else L, carried=prompt, tiles_per_seq=tps),
        grid=(M // tm, ncb),
        in_specs=in_specs,
        out_specs=[pl.BlockSpec((tm, tn), lambda i, j: (i, j)), pl.BlockSpec((gate_blk, tn), lambda i, j: (i, j))],
        out_shape=[jax.ShapeDtypeStruct((M, F), BF16), jax.ShapeDtypeStruct((gate_rows, F), F32)],
        scratch_shapes=[pltpu.VMEM((tm, K), BF16)] + ([pltpu.VMEM((SUBLANES, K), BF16)] if prompt else []),
        compiler_params=_cparams("parallel", "arbitrary"),
        name="ffn_up_act",
    )(*args)
    if prompt:
        tail = gate.reshape(nseq, tps, SUBLANES, F)[:, tps - 1, SUBLANES - (width - 1):]
    else:
        tail = gate.reshape(nseq, L, F)[:, L - (width - 1):]
    return act, tail


def _ffn_act_kernel(g_ref, u_ref, prev_ref, cw_ref, cb_ref, o_ref, *, seg, carried):
    prev = prev_ref[...]
    if carried:
        prev = jnp.where(pl.program_id(2) == 0, 0.0, prev)
    gate = _causal_conv(g_ref[...], prev, cw_ref[...], cb_ref[...], seg, carried)
    o_ref[...] = (jax.nn.gelu(gate, approximate=True) * u_ref[...]).astype(o_ref.dtype)


def ffn_act(up, nseq, L, state_conv, conv_w, conv_b, *, rows=256, cb=1536):
    M = up.shape[0]
    F = conv_w.shape[1]
    cb = _tile(F, cb, LANES)
    ncb = F // cb
    if state_conv is None:
        R = _tile(L, rows)
        nch = L // R
        in_specs = [pl.BlockSpec((R, cb), lambda b, j, c: (b * nch + c, j)),
                    pl.BlockSpec((R, cb), lambda b, j, c: (b * nch + c, ncb + j)),
                    pl.BlockSpec((SUBLANES, cb),
                                 lambda b, j, c: (jnp.maximum((b * nch + c) * (R // SUBLANES) - 1, 0), j)),
                    pl.BlockSpec((conv_w.shape[0], cb), lambda b, j, c: (0, j)),
                    pl.BlockSpec((1, cb), lambda b, j, c: (0, j))]
        return pl.pallas_call(
            functools.partial(_ffn_act_kernel, seg=R, carried=True),
            grid=(nseq, ncb, nch),
            in_specs=in_specs,
            out_specs=pl.BlockSpec((R, cb), lambda b, j, c: (b * nch + c, j)),
            out_shape=jax.ShapeDtypeStruct((M, F), BF16),
            compiler_params=_cparams("parallel", "parallel", "arbitrary"),
            name="ffn_act_prompt",
        )(up, up, up, conv_w, conv_b)
    R = _tile(M, rows)
    prev = _pad_state_rows(state_conv)
    in_specs = [pl.BlockSpec((R, cb), lambda i, j: (i, j)),
                pl.BlockSpec((R, cb), lambda i, j: (i, ncb + j)),
                pl.BlockSpec((R, cb), lambda i, j: (i, j)),
                pl.BlockSpec((conv_w.shape[0], cb), lambda i, j: (0, j)),
                pl.BlockSpec((1, cb), lambda i, j: (0, j))]
    return pl.pallas_call(
        functools.partial(_ffn_act_kernel, seg=L, carried=False),
        grid=(M // R, ncb),
        in_specs=in_specs,
        out_specs=pl.BlockSpec((R, cb), lambda i, j: (i, j)),
        out_shape=jax.ShapeDtypeStruct((M, F), BF16),
        compiler_params=_cparams("parallel", "parallel"),
        name="ffn_act_sample",
    )(up, up, prev, conv_w, conv_b)


def _norm_kernel(x_ref, g_ref, o_ref):
    o_ref[...] = _rms(x_ref[...], g_ref[...])


def rms_norm_rows(x, g, *, tm=512):
    M, D = x.shape
    tm = _tile(M, tm)
    return pl.pallas_call(
        _norm_kernel,
        grid=(M // tm,),
        in_specs=[pl.BlockSpec((tm, D), lambda i: (i, 0)), pl.BlockSpec((1, D), lambda i: (0, 0))],
        out_specs=pl.BlockSpec((tm, D), lambda i: (i, 0)),
        out_shape=jax.ShapeDtypeStruct((M, D), F32),
        compiler_params=_cparams("parallel"),
        name="final_norm",
    )(x, g.reshape(1, D))


def _pad_cols(w, width):
    return jnp.pad(w, ((0, 0), (0, width - w.shape[1])))


def _rotate_half_cols(w):
    half = w.shape[-1] // 2
    return jnp.concatenate([-w[..., half:], w[..., :half]], axis=-1)


def _row(v, width=None):
    v = v.reshape(1, -1).astype(F32)
    return v if width is None else _pad_cols(v, width)


def _layer_weights(l, dims, w_in, p):
    D, W, qr, R, rope, inner, cdim, heads_s = (dims[k] for k in
                                                ("D", "lru", "q_rank", "kv_rank", "rope", "inner", "cdim", "ssd_heads"))
    splits = (W, qr, R, rope, inner, cdim, heads_s, 3 * D)
    offs = [0]
    for s in splits:
        offs.append(offs[-1] + s)
    w = w_in[l]
    w_lru, w_qc, w_ckv, w_kr, w_z, w_xbc, w_dt, w_gate = (w[:, offs[i]:offs[i + 1]] for i in range(8))
    small = jnp.concatenate([_pad_cols(w_kr, LANES), _pad_cols(_rotate_half_cols(w_kr), LANES),
                             _pad_cols(w_dt, 2 * LANES)], axis=1)
    assert small.shape[1] == PROJ_BLOCK
    w_main = jnp.concatenate([w_lru, w_z, w_xbc, w_gate, w_qc, w_ckv, small], axis=1).astype(BF16)
    cols, o = {}, 0
    for name, width in (("lru", W), ("z", inner), ("xbc", cdim), ("gate", 3 * D), ("qc", qr), ("ckv", R),
                        ("small", PROJ_BLOCK)):
        assert o % PROJ_BLOCK == 0
        cols[name] = o
        o += width

    H, nope = dims["heads"], dims["nope"]
    hw = nope + LANES
    wq = p["mla_w_qb"][l].reshape(qr, H, nope + rope)
    wq_n, wq_r = wq[..., :nope], wq[..., nope:]
    zpad = jnp.zeros((qr, H, LANES - rope), F32)
    wq_a = jnp.concatenate([wq_n, wq_r, zpad], axis=-1).reshape(qr, H * hw)
    wq_b = jnp.concatenate([_rotate_half_cols(wq_r), zpad], axis=-1).reshape(qr, H * LANES)
    w_uk = p["mla_w_uk"][l]
    w_uv = p["mla_w_uv"][l]
    wts = {
        "w_main": w_main,
        "norm_mix": p["norm_mix"][l],
        "lru_conv_w": p["lru_conv_w"][l], "lru_conv_b": _row(p["lru_conv_b"][l]),
        "lru_w_a": p["lru_w_a"][l].astype(BF16), "lru_w_x": p["lru_w_x"][l].astype(BF16),
        "lru_b_a": _row(p["lru_b_a"][l]), "lru_b_x": _row(p["lru_b_x"][l]), "lru_lambda": _row(p["lru_lambda"][l]),
        "mla_q_norm": p["mla_q_norm"][l], "mla_kv_norm": p["mla_kv_norm"][l],
        "wq": jnp.concatenate([wq_a, wq_b], axis=1).astype(BF16),
        "w_uk_flat": w_uk.reshape(R, H * nope).astype(BF16),
        "w_uv_flat": w_uv.reshape(R, -1).astype(BF16),
        "w_uk_heads": jnp.transpose(w_uk, (1, 2, 0)).astype(BF16),
        "w_uv_heads": jnp.transpose(w_uv, (1, 0, 2)).astype(BF16),
        "ssd_heads": heads_s,
        "ssd_conv_w": p["ssd_conv_w"][l], "ssd_conv_b": _row(p["ssd_conv_b"][l]),
        "ssd_dt_bias": _row(p["ssd_dt_bias"][l], LANES), "ssd_a_log": _row(p["ssd_a_log"][l], LANES),
        "ssd_d_x": _row(jnp.repeat(p["ssd_d"][l], SSD_HEAD_DIM)), "ssd_norm": _row(p["ssd_norm"][l]),
        "w_out": p["w_out"][l].astype(BF16),
        "norm_xa": p["norm_xa"][l], "norm_mem": p["norm_mem"][l],
        "xa_wq": p["xa_wq"][l].astype(BF16),
        "xa_wkv": jnp.concatenate([p["xa_wk"][l], p["xa_wv"][l]], axis=1).astype(BF16),
        "xa_wo": p["xa_wo"][l].astype(BF16),
        "norm_ffn": p["norm_ffn"][l],
        "ffn_w_up": p["ffn_w_up"][l].astype(BF16), "ffn_w_down": p["ffn_w_down"][l].astype(BF16),
        "ffn_conv_w": p["ffn_conv_w"][l], "ffn_conv_b": _row(p["ffn_conv_b"][l]),
    }
    return wts, cols


def _rope_tables(pos0, L, rope, nope, rows, qscale):
    inv = 1.0 / (ROPE_THETA ** (jnp.arange(0, rope, 2, dtype=F32) / rope))
    ang = (pos0 + jnp.arange(L)).astype(F32)[:, None] * inv[None, :]
    cos, sin = jnp.cos(ang), jnp.sin(ang)
    cos2, sin2 = jnp.concatenate([cos, cos], axis=1), jnp.concatenate([sin, sin], axis=1)
    zpad = jnp.zeros((L, LANES - rope), F32)
    ck, sk = jnp.concatenate([cos2, zpad], axis=1), jnp.concatenate([sin2, zpad], axis=1)
    cq = jnp.concatenate([jnp.ones((L, nope), F32), ck], axis=1) * qscale
    sq = jnp.concatenate([jnp.zeros((L, nope), F32), sk], axis=1) * qscale
    reps = max(rows // L, 1)
    return tuple(jnp.tile(t, (reps, 1)) for t in (cq, sq, ck, sk))


def _trunk_layer(x, nseq, L, wts, cols, dims, tables, state, mem, paged):
    D, W, R, rope, inner, cdim, F = (dims[k] for k in ("D", "lru", "kv_rank", "rope", "inner", "cdim", "d_ff"))
    H, nope = dims["heads"], dims["nope"]
    hw = nope + LANES
    prompt = state is None
    proj = mm(x, wts["w_main"], gain=wts["norm_mix"], name="in_proj")

    y_a = lru_mixer(proj, nseq, L, None if prompt else state["lru_h"], None if prompt else state["lru_conv"], wts)

    q2 = mm(proj, wts["wq"], gain=wts["mla_q_norm"], a_col=cols["qc"] // dims["q_rank"], name="q_proj")
    q_all, ckv, kr = mla_prep(q2, proj, tables, wts["mla_kv_norm"], cols)
    if prompt:
        kn = mm(ckv, wts["w_uk_flat"], out_dtype=BF16, name="k_up")
        v = mm(ckv, wts["w_uv_flat"], out_dtype=BF16, name="v_up")
        y_b = mha_prompt(q_all, kn, kr, v, nseq, L, H)
    else:
        q_lat = hmm(q_all, wts["w_uk_heads"], a_stride=hw // nope, out_dtype=BF16, name="q_absorb")
        o_lat = mqa_paged(q_lat, q_all, ckv, kr, paged["pool_c"], paged["pool_kt"], paged["layer"],
                          paged["page_table"], H)
        y_b = hmm(o_lat, wts["w_uv_heads"], name="v_up_heads")

    if prompt:
        y_c, ssd_h = ssd_mixer(proj, nseq, L, None, None, None, None, wts, cols)
    else:
        y_c, ssd_h = ssd_mixer(proj, nseq, L, state["ssd_h"], state["layer"], state["ssd_h_out"],
                               state["ssd_conv"], wts, cols)

    x = merge_out(proj, y_a, y_b, y_c, wts["w_out"], x, cols)

    q = mm(x, wts["xa_wq"], gain=wts["norm_xa"], name="xa_q")
    if prompt:
        o = cross_attend(q, mem, mem, 0, 1, nseq, L, dims["mem_len"], dims["xa_heads"])
    else:
        o = cross_attend_cached(q, mem[0], mem[1], mem[2], L)
    x = mm(o, wts["xa_wo"], res=x, name="xa_o")

    act, ffn_tail = ffn_up_act(x, wts["norm_ffn"], wts["ffn_w_up"], nseq, L, None if prompt else state["ffn_conv"],
                               wts["ffn_conv_w"], wts["ffn_conv_b"])
    x = mm(act, wts["ffn_w_down"], res=x, name="ffn_down")

    def tail(a, c0, width, n):
        return a.reshape(nseq, L, -1)[:, L - n:, c0:c0 + width]

    rows = (ckv.reshape(nseq, L, R), kr.reshape(nseq, L, LANES)[..., :rope],
            y_a.reshape(nseq, L, W)[:, L - 1], tail(proj, cols["lru"], W, 3),
            ssd_h, tail(proj, cols["xbc"], cdim, 3), ffn_tail)
    return x, rows


def kernel(x_prompt, x_sample, cache_ckv, cache_krope, cache_mem_k, cache_mem_v, state_lru_h, state_lru_conv, state_ssd_h, state_ssd_conv, state_ffn_conv, page_table, mem_prompt, norm_mix, w_in, lru_conv_w, lru_conv_b, lru_w_a, lru_b_a, lru_w_x, lru_b_x, lru_lambda, mla_q_norm, mla_w_qb, mla_kv_norm, mla_w_uk, mla_w_uv, ssd_conv_w, ssd_conv_b, ssd_dt_bias, ssd_a_log, ssd_d, ssd_norm, w_out, norm_xa, norm_mem, xa_wq, xa_wk, xa_wv, xa_wo, norm_ffn, ffn_w_up, ffn_conv_w, ffn_conv_b, ffn_w_down, norm_final):
    p = dict(norm_mix=norm_mix, lru_conv_w=lru_conv_w, lru_conv_b=lru_conv_b, lru_w_a=lru_w_a, lru_b_a=lru_b_a,
             lru_w_x=lru_w_x, lru_b_x=lru_b_x, lru_lambda=lru_lambda, mla_q_norm=mla_q_norm, mla_w_qb=mla_w_qb,
             mla_kv_norm=mla_kv_norm, mla_w_uk=mla_w_uk, mla_w_uv=mla_w_uv, ssd_conv_w=ssd_conv_w,
             ssd_conv_b=ssd_conv_b, ssd_dt_bias=ssd_dt_bias, ssd_a_log=ssd_a_log, ssd_d=ssd_d, ssd_norm=ssd_norm,
             w_out=w_out, norm_xa=norm_xa, norm_mem=norm_mem, xa_wq=xa_wq, xa_wk=xa_wk, xa_wv=xa_wv, xa_wo=xa_wo,
             norm_ffn=norm_ffn, ffn_w_up=ffn_w_up, ffn_conv_w=ffn_conv_w, ffn_conv_b=ffn_conv_b,
             ffn_w_down=ffn_w_down)
    depth = w_in.shape[0]
    bp, lp, D = x_prompt.shape
    bs, ls, _ = x_sample.shape
    n_pages = page_table.shape[1]
    page = cache_ckv.shape[2]
    mem_len, xa_heads, xa_hd = cache_mem_k.shape[2:]
    R, H, nope = mla_w_uk.shape[1:]
    dims = dict(D=D, lru=lru_conv_w.shape[-1], q_rank=mla_q_norm.shape[-1], kv_rank=R, rope=cache_krope.shape[-1],
                inner=ssd_norm.shape[-1], cdim=ssd_conv_w.shape[-1], ssd_heads=ssd_a_log.shape[-1],
                d_ff=ffn_conv_w.shape[-1], heads=H, nope=nope, mem_len=mem_len, xa_heads=xa_heads)
    xa_w = xa_heads * xa_hd
    qscale = float(nope + dims["rope"]) ** -0.5 * LOG2_E
    tab_p = _rope_tables(0, lp, dims["rope"], nope, lp, qscale)
    tab_s = _rope_tables(n_pages * page, ls, dims["rope"], nope, min(256, bs * ls), qscale)
    pool_kt = jnp.swapaxes(cache_krope, 2, 3)

    xp = x_prompt.reshape(bp * lp, D)
    xs = x_sample.reshape(bs * ls, D)
    mem_rows = mem_prompt.reshape(bp * mem_len, D)
    p_rows, s_rows, p_mem_k, p_mem_v = [], [], [], []
    s_ssd_h = None
    for l in range(depth):
        wts, cols = _layer_weights(l, dims, w_in, p)
        mkv = mm(mem_rows, wts["xa_wkv"], gain=wts["norm_mem"], name="mem_kv")
        p_mem_k.append(mkv[:, :xa_w].reshape(bp, mem_len, xa_heads, xa_hd))
        p_mem_v.append(mkv[:, xa_w:].reshape(bp, mem_len, xa_heads, xa_hd))
        xp, rows = _trunk_layer(xp, bp, lp, wts, cols, dims, tab_p, None, mkv, None)
        p_rows.append(rows)
        state = dict(lru_h=state_lru_h[l], lru_conv=state_lru_conv[l], ssd_h=state_ssd_h, layer=l,
                     ssd_h_out=s_ssd_h, ssd_conv=state_ssd_conv[l], ffn_conv=state_ffn_conv[l])
        paged = dict(pool_c=cache_ckv, pool_kt=pool_kt, layer=l, page_table=page_table)
        xs, rows = _trunk_layer(xs, bs, ls, wts, cols, dims, tab_s, state, (cache_mem_k, cache_mem_v, l), paged)
        s_ssd_h = rows[4]
        s_rows.append(rows)
    y_prompt = rms_norm_rows(xp, norm_final).reshape(bp, lp, D)
    y_sample = rms_norm_rows(xs, norm_final).reshape(bs, ls, D)
    p_out = [jnp.stack(r) for r in zip(*p_rows)]
    s_out = [s_ssd_h if i == 4 else jnp.stack(r) for i, r in enumerate(zip(*s_rows))]
    return (y_prompt, y_sample, *p_out, jnp.stack(p_mem_k), jnp.stack(p_mem_v), *s_out)
```

```python
import functools

import jax
import jax.numpy as jnp
from jax import lax
from jax.experimental import pallas as pl
from jax.experimental.pallas import tpu as pltpu

F32 = jnp.float32
BF16 = jnp.bfloat16
EPS = 1e-6
LRU_C = 8.0
ROPE_THETA = 10000.0
LANES = 128
SUBLANES = 8
VMEM_LIMIT_BYTES = 56 * 2**20
SSD_CHUNK = 128
SSD_STATE = 128
SSD_HEAD_DIM = 64
LRU_BLOCK = 128
PAGES_PER_STEP = 32
PAGES_PER_GROUP = 32
LOG2_E = 1.4426950408889634
PROJ_BLOCK = 512


def _cparams(*sem):
    return pltpu.CompilerParams(dimension_semantics=sem, vmem_limit_bytes=VMEM_LIMIT_BYTES)


def _tile(n, pref, mult=SUBLANES):
    if n <= pref:
        return n
    t = pref - pref % mult
    while t > mult and n % t:
        t -= mult
    assert n % t == 0, (n, pref, mult)
    return t


def _rows(shape):
    return lax.broadcasted_iota(jnp.int32, shape, 0)


def _cols(shape):
    return lax.broadcasted_iota(jnp.int32, shape, 1)


def _rms(x, g):
    return x * lax.rsqrt(jnp.mean(x * x, axis=-1, keepdims=True) + EPS) * g


def _sigmoid(x):
    return 0.5 * jnp.tanh(0.5 * x) + 0.5


def _silu(x):
    return x * _sigmoid(x)


def _bf16_terms(v):
    v1 = v.astype(BF16)
    r1 = v - v1.astype(F32)
    v2 = r1.astype(BF16)
    v3 = (r1 - v2.astype(F32)).astype(BF16)
    return jnp.concatenate([v1, v2, v3], axis=1)


def _mm_kernel(*refs, has_gain, has_res, stage):
    it = iter(refs)
    a_ref = next(it)
    g_ref = next(it) if has_gain else None
    w_ref = next(it)
    r_ref = next(it) if has_res else None
    o_ref = next(it)
    h_ref = next(it) if stage else None
    if stage:
        @pl.when(pl.program_id(1) == 0)
        def _():
            a = a_ref[...].astype(F32)
            if has_gain:
                a = _rms(a, g_ref[...])
            h_ref[...] = a.astype(BF16)
        h = h_ref[...]
    else:
        h = a_ref[...]
    acc = jnp.dot(h, w_ref[...], preferred_element_type=F32)
    if has_res:
        acc = acc + r_ref[...]
    o_ref[...] = acc.astype(o_ref.dtype)


def mm(a, w, *, gain=None, res=None, a_col=0, layer=None, out_dtype=F32, tm=1024, tn=512, name="mm"):
    M = a.shape[0]
    K, N = w.shape[-2:]
    tm = _tile(M, tm)
    tn = _tile(N, tn, LANES)
    has_gain, has_res = gain is not None, res is not None
    stage = has_gain or a.dtype != BF16
    in_specs = [pl.BlockSpec((tm, K), lambda i, j: (i, a_col))]
    args = [a]
    if has_gain:
        in_specs.append(pl.BlockSpec((1, K), lambda i, j: (0, 0)))
        args.append(gain.reshape(1, K).astype(F32))
    if layer is None:
        in_specs.append(pl.BlockSpec((K, tn), lambda i, j: (0, j)))
    else:
        in_specs.append(pl.BlockSpec((None, K, tn), lambda i, j: (layer, 0, j)))
    args.append(w)
    if has_res:
        in_specs.append(pl.BlockSpec((tm, tn), lambda i, j: (i, j)))
        args.append(res)
    return pl.pallas_call(
        functools.partial(_mm_kernel, has_gain=has_gain, has_res=has_res, stage=stage),
        grid=(M // tm, N // tn),
        in_specs=in_specs,
        out_specs=pl.BlockSpec((tm, tn), lambda i, j: (i, j)),
        out_shape=jax.ShapeDtypeStruct((M, N), out_dtype),
        scratch_shapes=[pltpu.VMEM((tm, K), BF16)] if stage else [],
        compiler_params=_cparams("parallel", "arbitrary"),
        name=name,
    )(*args)


def _hmm_kernel(a_ref, w_ref, o_ref):
    o_ref[...] = jnp.dot(a_ref[...].astype(BF16), w_ref[...], preferred_element_type=F32).astype(o_ref.dtype)


def hmm(a, w, *, a_stride=1, out_dtype=F32, tm=1024, name="hmm"):
    M = a.shape[0]
    H, Ka, Nb = w.shape
    tm = _tile(M, tm)
    return pl.pallas_call(
        _hmm_kernel,
        grid=(M // tm, H),
        in_specs=[pl.BlockSpec((tm, Ka), lambda i, h: (i, h * a_stride)),
                  pl.BlockSpec((None, Ka, Nb), lambda i, h: (h, 0, 0))],
        out_specs=pl.BlockSpec((tm, Nb), lambda i, h: (i, h)),
        out_shape=jax.ShapeDtypeStruct((M, H * Nb), out_dtype),
        compiler_params=_cparams("parallel", "parallel"),
        name=name,
    )(a, w)


def _shift_rows(x, s, width, prev, seg, halo):
    R = x.shape[0]
    xr = pltpu.roll(x, s, 0)
    if not halo:
        up = width - 1 - s
        pr = pltpu.roll(prev, (R - up) % R, 0) if up else prev
        return jnp.where(_rows(x.shape) % seg >= s, xr, pr)
    pr = pltpu.roll(prev, s, 0)
    top = jnp.where(_rows(pr.shape) >= s, xr[:SUBLANES], pr)
    if R == SUBLANES:
        return top
    return jnp.concatenate([top, xr[SUBLANES:]], axis=0)


def _causal_conv(x, prev, w, b, seg, halo):
    width = w.shape[0]
    y = x * w[width - 1:width]
    for s in range(1, width):
        y = y + _shift_rows(x, s, width, prev, seg, halo) * w[width - 1 - s:width - s]
    return y + b


def _pad_state_rows(buf):
    nb, wm1, c = buf.shape
    return jnp.pad(buf, ((0, 0), (0, SUBLANES - wm1), (0, 0))).reshape(nb * SUBLANES, c)


def _segment_scan(a, u, seg):
    rid = _rows(a.shape) % seg
    d = 1
    while d < seg:
        ok = rid >= d
        a_sh = pltpu.roll(a, d, 0)
        u_sh = pltpu.roll(u, d, 0)
        u = jnp.where(ok, a * u_sh + u, u)
        a = jnp.where(ok, a * a_sh, a)
        d *= 2
    return a, u


def _segment_cumsum(x, seg):
    rid = _rows(x.shape) % seg
    d = 1
    while d < seg:
        x = jnp.where(rid >= d, x + pltpu.roll(x, d, 0), x)
        d *= 2
    return x


def _lru_kernel(*refs, seg, carried):
    if carried:
        x_ref, prev_ref, cw_ref, cb_ref, wa_ref, wx_ref, ba_ref, bx_ref, lam_ref, y_ref, carry_ref = refs
        h0_ref = None
        first = pl.program_id(2) == 0

        @pl.when(first)
        def _():
            carry_ref[...] = jnp.zeros_like(carry_ref)
    else:
        x_ref, prev_ref, cw_ref, cb_ref, wa_ref, wx_ref, ba_ref, bx_ref, lam_ref, h0_ref, y_ref = refs
    for k in range(x_ref.shape[1] // LRU_BLOCK):
        sl = slice(k * LRU_BLOCK, (k + 1) * LRU_BLOCK)
        x = x_ref[:, sl]
        prev = prev_ref[:, sl]
        if carried:
            prev = jnp.where(first, 0.0, prev)
        xc = _causal_conv(x, prev, cw_ref[:, sl], cb_ref[:, sl], seg, carried)
        xb = xc.astype(BF16)
        r = _sigmoid(jnp.dot(xb, wa_ref[k], preferred_element_type=F32) + ba_ref[:, sl])
        i = _sigmoid(jnp.dot(xb, wx_ref[k], preferred_element_type=F32) + bx_ref[:, sl])
        lam = lam_ref[:, sl]
        softplus_neg_lam = jnp.maximum(-lam, 0.0) + jnp.log1p(jnp.exp(-jnp.abs(lam)))
        log_a = -LRU_C * r * softplus_neg_lam
        a = jnp.exp(log_a)
        u = jnp.sqrt(-jnp.tanh(log_a) * (a * a + 1.0)) * (i * xc)
        if not carried:
            u = u + a * h0_ref[:, sl]
            y_ref[:, sl] = _segment_scan(a, u, seg)[1]
        else:
            a8, u8 = _segment_scan(a, u, SUBLANES)
            hprev = carry_ref[:, sl]
            for g in range(seg // SUBLANES):
                gs = slice(g * SUBLANES, (g + 1) * SUBLANES)
                hg = u8[gs] + a8[gs] * hprev
                y_ref[gs, sl] = hg
                hprev = hg[SUBLANES - 1:SUBLANES]
            carry_ref[:, sl] = hprev


def lru_mixer(proj, nseq, L, state_h, state_conv, wts, *, cb=512, rows=256):
    M = proj.shape[0]
    W = wts["lru_conv_w"].shape[1]
    cb = _tile(W, cb, LANES)
    nb = cb // LRU_BLOCK
    common = [wts["lru_conv_w"], wts["lru_conv_b"], wts["lru_w_a"], wts["lru_w_x"], wts["lru_b_a"],
              wts["lru_b_x"], wts["lru_lambda"]]
    if state_h is None:
        R = _tile(L, rows)
        nch = L // R

        def vec(n):
            return pl.BlockSpec((n, cb), lambda b, j, c: (0, j))

        def gates():
            return pl.BlockSpec((nb, LRU_BLOCK, LRU_BLOCK), lambda b, j, c: (j, 0, 0))

        in_specs = [
            pl.BlockSpec((R, cb), lambda b, j, c: (b * nch + c, j)),
            pl.BlockSpec((SUBLANES, cb), lambda b, j, c: (jnp.maximum((b * nch + c) * (R // SUBLANES) - 1, 0), j)),
            vec(4), vec(1), gates(), gates(), vec(1), vec(1), vec(1),
        ]
        return pl.pallas_call(
            functools.partial(_lru_kernel, seg=R, carried=True),
            grid=(nseq, W // cb, nch),
            in_specs=in_specs,
            out_specs=pl.BlockSpec((R, cb), lambda b, j, c: (b * nch + c, j)),
            out_shape=jax.ShapeDtypeStruct((M, W), F32),
            scratch_shapes=[pltpu.VMEM((1, cb), F32)],
            compiler_params=_cparams("parallel", "parallel", "arbitrary"),
            name="lru_prompt",
        )(proj, proj, *common)
    R = _tile(M, rows)
    h0_rows = jnp.pad(state_h[:, None, :], ((0, 0), (0, L - 1), (0, 0))).reshape(M, W)
    prev = _pad_state_rows(state_conv)

    def vec(n):
        return pl.BlockSpec((n, cb), lambda i, j: (0, j))

    def tile():
        return pl.BlockSpec((R, cb), lambda i, j: (i, j))

    def gates():
        return pl.BlockSpec((nb, LRU_BLOCK, LRU_BLOCK), lambda i, j: (j, 0, 0))

    in_specs = [tile(), tile(), vec(4), vec(1), gates(), gates(), vec(1), vec(1), vec(1), tile()]
    return pl.pallas_call(
        functools.partial(_lru_kernel, seg=L, carried=False),
        grid=(M // R, W // cb),
        in_specs=in_specs,
        out_specs=tile(),
        out_shape=jax.ShapeDtypeStruct((M, W), F32),
        compiler_params=_cparams("parallel", "parallel"),
        name="lru_sample",
    )(proj, prev, *common, h0_rows)


def _mla_prep_kernel(qa_ref, qb_ref, ckv_ref, kr_ref, cq_ref, sq_ref, ck_ref, sk_ref, g_ref,
                     q_ref, ckvn_ref, krr_ref):
    hw = cq_ref.shape[1]
    nope = hw - LANES
    cq, sq = cq_ref[...], sq_ref[:, nope:]
    for h in range(q_ref.shape[1] // hw):
        sl = slice(h * hw, (h + 1) * hw)
        qa = qa_ref[:, sl] * cq
        rot = qb_ref[:, h * LANES:(h + 1) * LANES] * sq
        q_ref[:, sl] = jnp.concatenate([qa[:, :nope], qa[:, nope:] + rot], axis=1).astype(q_ref.dtype)
    ckvn_ref[...] = _rms(ckv_ref[...], g_ref[...])
    kr = kr_ref[...]
    krr_ref[...] = kr[:, :LANES] * ck_ref[...] + kr[:, LANES:2 * LANES] * sk_ref[...]


def mla_prep(q2, proj, tables, kv_gain, cols, *, tm=256):
    M = q2.shape[0]
    cq, sq, ck, sk = tables
    hw = cq.shape[1]
    heads = q2.shape[1] // (hw + LANES)
    qw, qbw = heads * hw, heads * LANES
    assert qw % qbw == 0
    tm = _tile(min(M, cq.shape[0]), tm)
    nt = cq.shape[0] // tm
    R = kv_gain.shape[0]

    def tab(w):
        return pl.BlockSpec((tm, w), lambda i: (i % nt, 0))

    return pl.pallas_call(
        _mla_prep_kernel,
        grid=(M // tm,),
        in_specs=[pl.BlockSpec((tm, qw), lambda i: (i, 0)), pl.BlockSpec((tm, qbw), lambda i: (i, qw // qbw)),
                  pl.BlockSpec((tm, R), lambda i: (i, cols["ckv"] // R)),
                  pl.BlockSpec((tm, PROJ_BLOCK), lambda i: (i, cols["small"] // PROJ_BLOCK)),
                  tab(cq.shape[1]), tab(cq.shape[1]), tab(LANES), tab(LANES),
                  pl.BlockSpec((1, R), lambda i: (0, 0))],
        out_specs=[pl.BlockSpec((tm, qw), lambda i: (i, 0)), pl.BlockSpec((tm, R), lambda i: (i, 0)),
                   pl.BlockSpec((tm, LANES), lambda i: (i, 0))],
        out_shape=[jax.ShapeDtypeStruct((M, qw), BF16), jax.ShapeDtypeStruct((M, R), F32),
                   jax.ShapeDtypeStruct((M, LANES), F32)],
        compiler_params=_cparams("parallel"),
        name="mla_prep",
    )(q2, q2, proj, proj, cq, sq, ck, sk, kv_gain.reshape(1, R))


def _mha_kernel(q_ref, kn_ref, kr_ref, v_ref, o_ref, kcat_ref, vaug_ref, *, tk, nq):
    qi = pl.program_id(2)
    dk = kn_ref.shape[1]
    dv = v_ref.shape[1]

    @pl.when(qi == 0)
    def _():
        kcat_ref[:, :dk] = kn_ref[...]
        kcat_ref[:, dk:] = kr_ref[...].astype(BF16)
        vaug_ref[:, :dv] = v_ref[...]
        vaug_ref[:, dv:] = jnp.ones((vaug_ref.shape[0], vaug_ref.shape[1] - dv), BF16)

    q = q_ref[...]
    dn = (((1,), (1,)), ((), ()))
    for n in range(nq):
        @pl.when(qi == n)
        def _(n=n):
            lo, hi = n * tk, (n + 1) * tk
            sd = lax.dot_general(q, kcat_ref[lo:hi, :], dn, preferred_element_type=F32)
            sd = jnp.where(_cols(sd.shape) <= _rows(sd.shape), sd, -jnp.inf)
            m = jnp.max(sd, axis=-1, keepdims=True)
            if n:
                sp = lax.dot_general(q, kcat_ref[0:lo, :], dn, preferred_element_type=F32)
                m = jnp.maximum(m, jnp.max(sp, axis=-1, keepdims=True))
            acc = jnp.dot(jnp.exp2(sd - m).astype(BF16), vaug_ref[lo:hi, :], preferred_element_type=F32)
            if n:
                acc = acc + jnp.dot(jnp.exp2(sp - m).astype(BF16), vaug_ref[0:lo, :], preferred_element_type=F32)
            o_ref[...] = (acc[:, :dv] / acc[:, dv:2 * dv]).astype(o_ref.dtype)


def mha_prompt(q, kn, kr, v, nseq, L, heads, *, tq=512):
    M = q.shape[0]
    hw = q.shape[1] // heads
    dv = v.shape[1] // heads
    tq = _tile(L, tq)
    nq = L // tq
    return pl.pallas_call(
        functools.partial(_mha_kernel, tk=tq, nq=nq),
        grid=(nseq, heads, nq),
        in_specs=[pl.BlockSpec((tq, hw), lambda b, h, i: (b * nq + i, h)),
                  pl.BlockSpec((L, hw - LANES), lambda b, h, i: (b, h)),
                  pl.BlockSpec((L, LANES), lambda b, h, i: (b, 0)),
                  pl.BlockSpec((L, dv), lambda b, h, i: (b, h))],
        out_specs=pl.BlockSpec((tq, dv), lambda b, h, i: (b * nq + i, h)),
        out_shape=jax.ShapeDtypeStruct((M, heads * dv), F32),
        scratch_shapes=[pltpu.VMEM((L, hw), BF16), pltpu.VMEM((L, 2 * dv), BF16)],
        compiler_params=_cparams("parallel", "parallel", "arbitrary"),
        name="mha_prompt",
    )(q, kn, kr, v)


def _mqa_paged_kernel(pt_ref, ql_ref, qr_ref, *refs, pps, group, rope, heads):
    del pt_ref
    c_refs = refs[:pps]
    k_refs = refs[pps:2 * pps]
    cn_ref, kn_ref, o_ref, m_ref, l_ref, acc_ref, cbuf_ref, kbuf_ref = refs[2 * pps:]
    step = pl.program_id(1)
    page = c_refs[0].shape[0]

    @pl.when(step == 0)
    def _():
        m_ref[...] = jnp.full_like(m_ref, -jnp.inf)
        l_ref[...] = jnp.zeros_like(l_ref)
        acc_ref[...] = jnp.zeros_like(acc_ref)

    ql = ql_ref[...]
    qr = qr_ref[:, LANES:LANES + rope]
    dn = (((1,), (1,)), ((), ()))

    def update(s, c):
        m = m_ref[...]
        m_new = jnp.maximum(m, jnp.max(s, axis=-1, keepdims=True))
        alpha = jnp.exp2(m - m_new)
        p = jnp.exp2(s - m_new)
        m_ref[...] = m_new
        l_ref[...] = alpha * l_ref[...] + jnp.sum(p, axis=-1, keepdims=True)
        acc_ref[...] = alpha * acc_ref[...] + jnp.dot(p.astype(BF16), c, preferred_element_type=F32)

    for k, (c_ref, k_ref) in enumerate(zip(c_refs, k_refs)):
        cbuf_ref[k * page:(k + 1) * page, :] = c_ref[...].astype(BF16)
        kbuf_ref[:, k * page:(k + 1) * page] = k_ref[...].astype(BF16)
    for g0 in range(0, pps, group):
        c = cbuf_ref[g0 * page:(g0 + group) * page, :]
        s = lax.dot_general(ql, c, dn, preferred_element_type=F32)
        s = s + jnp.dot(qr, kbuf_ref[:, g0 * page:(g0 + group) * page], preferred_element_type=F32)
        update(s, c)

    @pl.when(step == pl.num_programs(1) - 1)
    def _():
        c = cn_ref[...].astype(BF16)
        s = lax.dot_general(ql, c, dn, preferred_element_type=F32)
        s = s + lax.dot_general(qr, kn_ref[:, :rope].astype(BF16), dn, preferred_element_type=F32)
        s = jnp.where(_cols(s.shape) <= _rows(s.shape) // heads, s, -jnp.inf)
        update(s, c)
        o_ref[...] = (acc_ref[...] / l_ref[...]).astype(o_ref.dtype)


def mqa_paged(q_lat, q_all, ckv_new, kr_new, pool_c, pool_kt, layer, page_table, heads):
    nb, n_pages = page_table.shape
    T = ckv_new.shape[0] // nb
    R = pool_c.shape[-1]
    rope = pool_kt.shape[2]
    page = pool_c.shape[2]
    pps = _tile(n_pages, PAGES_PER_STEP, 1)
    group = _tile(pps, PAGES_PER_GROUP, 1)
    rows = T * heads
    ql = q_lat.reshape(nb * rows, R)
    qa = q_all.reshape(nb * rows, q_all.shape[1] // heads)

    def page_spec(shape, k):
        return pl.BlockSpec((None, None) + shape, lambda b, s, pt: (layer, pt[b, s * pps + k], 0, 0))

    grid_spec = pltpu.PrefetchScalarGridSpec(
        num_scalar_prefetch=1,
        grid=(nb, n_pages // pps),
        in_specs=[pl.BlockSpec((rows, R), lambda b, s, pt: (b, 0)),
                  pl.BlockSpec((rows, qa.shape[1]), lambda b, s, pt: (b, 0))]
                 + [page_spec((page, R), k) for k in range(pps)]
                 + [page_spec((rope, page), k) for k in range(pps)]
                 + [pl.BlockSpec((T, R), lambda b, s, pt: (b, 0)),
                    pl.BlockSpec((T, LANES), lambda b, s, pt: (b, 0))],
        out_specs=pl.BlockSpec((rows, R), lambda b, s, pt: (b, 0)),
        scratch_shapes=[pltpu.VMEM((rows, 1), F32), pltpu.VMEM((rows, 1), F32), pltpu.VMEM((rows, R), F32),
                        pltpu.VMEM((pps * page, R), BF16), pltpu.VMEM((rope, pps * page), BF16)],
    )
    out = pl.pallas_call(
        functools.partial(_mqa_paged_kernel, pps=pps, group=group, rope=rope, heads=heads),
        grid_spec=grid_spec,
        out_shape=jax.ShapeDtypeStruct((nb * rows, R), BF16),
        compiler_params=_cparams("parallel", "arbitrary"),
        name="mqa_paged",
    )(page_table, ql, qa, *([pool_c] * pps), *([pool_kt] * pps), ckv_new, kr_new)
    return out.reshape(nb * T, heads * R)


def _ssd_kernel(*refs, seg, carried, aliased, heads, conv_cb):
    if carried:
        (z_ref, xbc_ref, prev_ref, dt_ref, cw_ref, cb_ref, dtb_ref, alog_ref, dskx_ref, ng_ref, e_ref,
         y_ref, hout_ref, xc_ref, xd_ref, xdw_ref, yacc_ref, yoff_ref, al_ref, st_ref) = refs
        h0_ref = None
        first = pl.program_id(1) == 0

        @pl.when(first)
        def _():
            st_ref[...] = jnp.zeros_like(st_ref)
    else:
        refs = refs[:12] + refs[12 + int(aliased):]
        (z_ref, xbc_ref, prev_ref, dt_ref, cw_ref, cb_ref, dtb_ref, alog_ref, dskx_ref, ng_ref, e_ref, h0_ref,
         y_ref, hout_ref, xc_ref, xd_ref, xdw_ref, yacc_ref, yoff_ref, al_ref) = refs
    R = z_ref.shape[0]
    inner = z_ref.shape[1]
    P = inner // heads
    N = SSD_STATE
    G = (xbc_ref.shape[1] - inner) // (2 * N)
    hpg = heads // G
    gw = hpg * P
    dn_t = (((1,), (1,)), ((), ()))
    exact = lax.Precision.HIGHEST

    for k in range(xbc_ref.shape[1] // conv_cb):
        sl = slice(k * conv_cb, (k + 1) * conv_cb)
        prev = prev_ref[:, sl]
        if carried:
            prev = jnp.where(first, 0.0, prev)
        xc_ref[:, sl] = _silu(_causal_conv(xbc_ref[:, sl], prev, cw_ref[:, sl], cb_ref[:, sl], seg, carried))

    dt = jax.nn.softplus(dt_ref[...] + dtb_ref[...])
    acum = _segment_cumsum(dt * -jnp.exp(alog_ref[...]), seg)
    rid, cid = _rows((R, R)), _cols((R, R))
    seg_last = (cid == (rid // seg) * seg + (seg - 1)).astype(F32)
    alast = jnp.dot(seg_last, acum, precision=exact, preferred_element_type=F32)
    al_ref[...] = alast
    e_mat = e_ref[...]

    def per_channel(v):
        return jnp.dot(_bf16_terms(v), e_mat, preferred_element_type=F32)

    xd = xc_ref[:, :inner] * per_channel(dt)
    xd_ref[...] = xd
    xdw_ref[...] = xd * per_channel(jnp.exp(alast - acum))

    if R < LANES:
        acum_sq = jnp.concatenate([acum, jnp.zeros((LANES - R, LANES), F32)], axis=0)
    else:
        acum_sq = acum
    acum_t = acum_sq.T
    allowed = (cid <= rid) & (cid // seg == rid // seg)

    for g in range(G):
        bm = xc_ref[:, inner + g * N: inner + (g + 1) * N].astype(BF16)
        cm = xc_ref[:, inner + (G + g) * N: inner + (G + g + 1) * N].astype(BF16)
        cb_mat = lax.dot_general(cm, bm, dn_t, preferred_element_type=F32)
        for e in range(hpg):
            h = g * hpg + e
            decay = jnp.exp(jnp.where(allowed, acum[:, h:h + 1] - acum_t[h:h + 1, :R], -jnp.inf))
            yacc_ref[:, h * P:(h + 1) * P] = jnp.dot((cb_mat * decay).astype(BF16),
                                                     xd_ref[:, h * P:(h + 1) * P].astype(BF16),
                                                     preferred_element_type=F32)

    def per_segment(s, carry):
        rows = pl.ds(pl.multiple_of(s * seg, seg), seg)
        grow = jnp.exp(al_ref[pl.ds(pl.multiple_of(s * seg, seg), 1), :])
        for g in range(G):
            gs = slice(g * gw, (g + 1) * gw)
            bm = xc_ref[rows, inner + g * N: inner + (g + 1) * N].astype(BF16)
            cm = xc_ref[rows, inner + (G + g) * N: inner + (G + g + 1) * N].astype(BF16)
            hprev = st_ref[gs, :] if carried else h0_ref[s, gs, :]
            yoff_ref[rows, gs] = lax.dot_general(cm, hprev.astype(BF16), dn_t, preferred_element_type=F32)
            st = lax.dot_general(xdw_ref[rows, gs].astype(BF16), bm, (((0,), (0,)), ((), ())),
                                 preferred_element_type=F32)
            for e in range(hpg):
                h = g * hpg + e
                hnew = hprev[e * P:(e + 1) * P] * grow[:, h:h + 1] + st[e * P:(e + 1) * P]
                if carried:
                    st_ref[h * P:(h + 1) * P, :] = hnew
                    hout_ref[h * P:(h + 1) * P, :] = hnew
                else:
                    hout_ref[s, h * P:(h + 1) * P, :] = hnew
        return carry

    if R == seg:
        per_segment(0, 0)
    else:
        lax.fori_loop(0, R // seg, per_segment, 0)

    y = yacc_ref[...] + yoff_ref[...] * per_channel(jnp.exp(acum)) + dskx_ref[...] * xc_ref[:, :inner]
    y_ref[...] = _rms(y * _silu(z_ref[...]), ng_ref[...])


def ssd_mixer(proj, nseq, L, state_h, layer, prev_out, state_conv, wts, cols, *, seqs_per_tile=8):
    M = proj.shape[0]
    heads = wts["ssd_heads"]
    P, N = SSD_HEAD_DIM, SSD_STATE
    inner = heads * P
    cdim = wts["ssd_conv_w"].shape[1]
    conv_cb = PROJ_BLOCK
    zc, xc, dc = cols["z"] // inner, cols["xbc"] // cdim, (cols["small"] + 2 * LANES) // LANES
    assert cols["z"] % inner == 0 and cols["xbc"] % cdim == 0
    e_mat = (jnp.arange(3 * LANES)[:, None] % LANES == jnp.arange(inner)[None, :] // P).astype(BF16)
    vecs = [wts["ssd_conv_w"], wts["ssd_conv_b"], wts["ssd_dt_bias"], wts["ssd_a_log"], wts["ssd_d_x"],
            wts["ssd_norm"], e_mat]
    widths = [(4, cdim), (1, cdim), (1, LANES), (1, LANES), (1, inner), (1, inner), (3 * LANES, inner)]

    def scratch(R):
        return [pltpu.VMEM((R, cdim), F32), pltpu.VMEM((R, inner), F32), pltpu.VMEM((R, inner), F32),
                pltpu.VMEM((R, inner), F32), pltpu.VMEM((R, inner), F32), pltpu.VMEM((R, LANES), F32)]

    if state_h is None:
        Q = SSD_CHUNK if L % SSD_CHUNK == 0 else L
        nch = L // Q
        in_specs = [pl.BlockSpec((Q, inner), lambda b, c: (b * nch + c, zc)),
                    pl.BlockSpec((Q, cdim), lambda b, c: (b * nch + c, xc)),
                    pl.BlockSpec((SUBLANES, cdim),
                                 lambda b, c: (jnp.maximum((b * nch + c) * (Q // SUBLANES) - 1, 0), xc)),
                    pl.BlockSpec((Q, LANES), lambda b, c: (b * nch + c, dc))]
        in_specs += [pl.BlockSpec(w, lambda b, c: (0, 0)) for w in widths]
        y, st = pl.pallas_call(
            functools.partial(_ssd_kernel, seg=Q, carried=True, aliased=False, heads=heads, conv_cb=conv_cb),
            grid=(nseq, nch),
            in_specs=in_specs,
            out_specs=[pl.BlockSpec((Q, inner), lambda b, c: (b * nch + c, 0)),
                       pl.BlockSpec((None, inner, N), lambda b, c: (b, 0, 0))],
            out_shape=[jax.ShapeDtypeStruct((M, inner), F32), jax.ShapeDtypeStruct((nseq, inner, N), F32)],
            scratch_shapes=scratch(Q) + [pltpu.VMEM((inner, N), F32)],
            compiler_params=_cparams("parallel", "arbitrary"),
            name="ssd_prompt",
        )(proj, proj, proj, proj, *vecs)
        return y, st.reshape(nseq, heads, P, N)
    depth = state_h.shape[0]
    ns = _tile(nseq, seqs_per_tile, 1)
    R = ns * L
    prev = _pad_state_rows(state_conv)
    in_specs = [pl.BlockSpec((R, inner), lambda i: (i, zc)),
                pl.BlockSpec((R, cdim), lambda i: (i, xc)),
                pl.BlockSpec((R, cdim), lambda i: (i, 0)),
                pl.BlockSpec((R, LANES), lambda i: (i, dc))]
    in_specs += [pl.BlockSpec(w, lambda i: (0, 0)) for w in widths]
    in_specs += [pl.BlockSpec((None, ns, inner, N), lambda i: (layer, i, 0, 0))]
    args = [proj, proj, prev, proj, *vecs, state_h.reshape(depth, nseq, inner, N)]
    aliases = {}
    if prev_out is not None:
        in_specs.append(pl.BlockSpec(memory_space=pl.ANY))
        args.append(prev_out.reshape(depth, nseq, inner, N))
        aliases = {len(args) - 1: 1}
    y, st = pl.pallas_call(
        functools.partial(_ssd_kernel, seg=L, carried=False, aliased=prev_out is not None, heads=heads,
                          conv_cb=conv_cb),
        grid=(M // R,),
        in_specs=in_specs,
        out_specs=[pl.BlockSpec((R, inner), lambda i: (i, 0)),
                   pl.BlockSpec((None, ns, inner, N), lambda i: (layer, i, 0, 0))],
        out_shape=[jax.ShapeDtypeStruct((M, inner), F32), jax.ShapeDtypeStruct((depth, nseq, inner, N), F32)],
        scratch_shapes=scratch(R),
        input_output_aliases=aliases,
        compiler_params=_cparams("parallel"),
        name="ssd_sample",
    )(*args)
    return y, st.reshape(depth, nseq, heads, P, N)


def _merge_kernel(g0_ref, g1_ref, g2_ref, ya_ref, yb_ref, yc_ref, w_ref, x_ref, o_ref, h_ref):
    @pl.when(pl.program_id(1) == 0)
    def _():
        merged = (_sigmoid(g0_ref[...]) * ya_ref[...] + _sigmoid(g1_ref[...]) * yb_ref[...]
                  + _sigmoid(g2_ref[...]) * yc_ref[...])
        h_ref[...] = merged.astype(BF16)
    o_ref[...] = x_ref[...] + jnp.dot(h_ref[...], w_ref[...], preferred_element_type=F32)


def merge_out(proj, ya, yb, yc, w_out, x, cols, *, tm=256, tn=2048):
    M, D = x.shape
    tm = _tile(M, tm)
    tn = _tile(D, tn, LANES)
    g0 = cols["gate"] // D
    assert cols["gate"] % D == 0

    def gate(k):
        return pl.BlockSpec((tm, D), lambda i, j: (i, g0 + k))

    def row():
        return pl.BlockSpec((tm, D), lambda i, j: (i, 0))

    return pl.pallas_call(
        _merge_kernel,
        grid=(M // tm, D // tn),
        in_specs=[gate(0), gate(1), gate(2), row(), row(), row(),
                  pl.BlockSpec((D, tn), lambda i, j: (0, j)), pl.BlockSpec((tm, tn), lambda i, j: (i, j))],
        out_specs=pl.BlockSpec((tm, tn), lambda i, j: (i, j)),
        out_shape=jax.ShapeDtypeStruct((M, D), F32),
        scratch_shapes=[pltpu.VMEM((tm, D), BF16)],
        compiler_params=_cparams("parallel", "arbitrary"),
        name="merge_out",
    )(proj, proj, proj, ya, yb, yc, w_out, x)


def _xattn_kernel(q_ref, k_ref, v_ref, o_ref, *, heads, scale):
    d = q_ref.shape[1] // heads
    for h in range(heads):
        sl = slice(h * d, (h + 1) * d)
        q = q_ref[:, sl].astype(BF16)
        sc = lax.dot_general(q, k_ref[:, sl].astype(BF16), (((1,), (1,)), ((), ())),
                             preferred_element_type=F32) * scale
        p = jnp.exp(sc - jnp.max(sc, axis=-1, keepdims=True))
        p = p / jnp.sum(p, axis=-1, keepdims=True)
        o_ref[:, sl] = jnp.dot(p.astype(BF16), v_ref[:, sl].astype(BF16),
                               preferred_element_type=F32).astype(o_ref.dtype)


def _xattn_cached_kernel(q_ref, k_ref, v_ref, o_ref, *, heads, scale, seqs):
    rows = q_ref.shape[0] // seqs
    for s in range(seqs):
        rs = slice(s * rows, (s + 1) * rows)
        q = q_ref[rs, :].astype(BF16)
        sc = lax.dot_general(q, k_ref[s].astype(BF16), (((1,), (1,)), ((), ())), preferred_element_type=F32) * scale
        sc = jnp.where(_cols(sc.shape) % heads == _rows(sc.shape) % heads, sc, -jnp.inf)
        p = jnp.exp(sc - jnp.max(sc, axis=-1, keepdims=True))
        p = p / jnp.sum(p, axis=-1, keepdims=True)
        o_ref[rs, :] = jnp.dot(p.astype(BF16), v_ref[s].astype(BF16), preferred_element_type=F32)


def cross_attend_cached(q, mem_k, mem_v, layer, L, *, seqs=8):
    M, W = q.shape
    depth, nseq, mem_len, heads, d = mem_k.shape
    seqs = _tile(nseq, seqs, 1)
    rows = seqs * L * heads

    def mem():
        return pl.BlockSpec((None, seqs, mem_len * heads, d), lambda i: (layer, i, 0, 0))

    out = pl.pallas_call(
        functools.partial(_xattn_cached_kernel, heads=heads, scale=d ** -0.5, seqs=seqs),
        grid=(nseq // seqs,),
        in_specs=[pl.BlockSpec((rows, d), lambda i: (i, 0)), mem(), mem()],
        out_specs=pl.BlockSpec((rows, d), lambda i: (i, 0)),
        out_shape=jax.ShapeDtypeStruct((M * heads, d), F32),
        compiler_params=_cparams("parallel"),
        name="cross_attend_cached",
    )(q.reshape(M * heads, d), mem_k.reshape(depth, nseq, mem_len * heads, d),
      mem_v.reshape(depth, nseq, mem_len * heads, d))
    return out.reshape(M, W)


def cross_attend(q, k, v, k_col, v_col, nseq, L, mem_len, heads, *, tq=512):
    M, W = q.shape
    tq = _tile(L, tq)
    nq = L // tq
    return pl.pallas_call(
        functools.partial(_xattn_kernel, heads=heads, scale=(W // heads) ** -0.5),
        grid=(nseq, nq),
        in_specs=[pl.BlockSpec((tq, W), lambda b, i: (b * nq + i, 0)),
                  pl.BlockSpec((mem_len, W), lambda b, i: (b, k_col)),
                  pl.BlockSpec((mem_len, W), lambda b, i: (b, v_col))],
        out_specs=pl.BlockSpec((tq, W), lambda b, i: (b * nq + i, 0)),
        out_shape=jax.ShapeDtypeStruct((M, W), F32),
        compiler_params=_cparams("parallel", "parallel"),
        name="cross_attend",
    )(q, k, v)


def _ffn_up_act_kernel(*refs, seg, carried, tiles_per_seq):
    if carried:
        x_ref, xh_ref, g_ref, wg_ref, wu_ref, cw_ref, cb_ref, act_ref, gate_ref, h_ref, hh_ref = refs
    else:
        x_ref, g_ref, wg_ref, wu_ref, cw_ref, cb_ref, prev_ref, act_ref, gate_ref, h_ref = refs

    @pl.when(pl.program_id(1) == 0)
    def _():
        h_ref[...] = _rms(x_ref[...], g_ref[...]).astype(BF16)
        if carried:
            hh_ref[...] = _rms(xh_ref[...], g_ref[...]).astype(BF16)

    h = h_ref[...]
    wg = wg_ref[...]
    gate = jnp.dot(h, wg, preferred_element_type=F32)
    up = jnp.dot(h, wu_ref[...], preferred_element_type=F32)
    if carried:
        prev = jnp.dot(hh_ref[...], wg, preferred_element_type=F32)
        prev = jnp.where(pl.program_id(0) % tiles_per_seq == 0, 0.0, prev)
        gate_ref[...] = gate[gate.shape[0] - SUBLANES:]
    else:
        prev = prev_ref[...]
        gate_ref[...] = gate
    conv = _causal_conv(gate, prev, cw_ref[...], cb_ref[...], seg, carried)
    act_ref[...] = (jax.nn.gelu(conv, approximate=True) * up).astype(act_ref.dtype)


def ffn_up_act(x, gain, w_up, layer, nseq, L, state_conv, conv_w, conv_b, *, tm=1024, tn=512):
    M, K = x.shape
    F = conv_w.shape[1]
    width = conv_w.shape[0]
    tn = _tile(F, tn, LANES)
    ncb = F // tn
    prompt = state_conv is None
    tm = _tile(L, tm) if prompt else _tile(M, tm)
    tps = L // tm if prompt else 1

    def wspec(off):
        return pl.BlockSpec((None, K, tn), lambda i, j: (layer, 0, off + j))

    def cspec(n):
        return pl.BlockSpec((n, tn), lambda i, j: (0, j))

    in_specs = [pl.BlockSpec((tm, K), lambda i, j: (i, 0))]
    args = [x]
    if prompt:
        in_specs.append(pl.BlockSpec((SUBLANES, K), lambda i, j: (jnp.maximum(i * (tm // SUBLANES) - 1, 0), 0)))
        args.append(x)
    in_specs += [pl.BlockSpec((1, K), lambda i, j: (0, 0)), wspec(0), wspec(ncb), cspec(width), cspec(1)]
    args += [gain.reshape(1, K), w_up, w_up, conv_w, conv_b]
    if prompt:
        gate_rows, gate_blk = (M // tm) * SUBLANES, SUBLANES
    else:
        in_specs.append(pl.BlockSpec((tm, tn), lambda i, j: (i, j)))
        args.append(_pad_state_rows(state_conv))
        gate_rows, gate_blk = M, tm
    act, gate = pl.pallas_call(
        functools.partial(_ffn_up_act_kernel, seg=tm if prompt else L, carried=prompt, tiles_per_seq=tps),
        grid=(M // tm, ncb),
        in_specs=in_specs,
        out_specs=[pl.BlockSpec((tm, tn), lambda i, j: (i, j)), pl.BlockSpec((gate_blk, tn), lambda i, j: (i, j))],
        out_shape=[jax.ShapeDtypeStruct((M, F), BF16), jax.ShapeDtypeStruct((gate_rows, F), F32)],
        scratch_shapes=[pltpu.VMEM((tm, K), BF16)] + ([pltpu.VMEM((SUBLANES, K), BF16)] if prompt else []),
        compiler_params=_cparams("parallel", "arbitrary"),
        name="ffn_up_act",
    )(*args)
    if prompt:
        tail = gate.reshape(nseq, tps, SUBLANES, F)[:, tps - 1, SUBLANES - (width - 1):]
    else:
        tail = gate.reshape(nseq, L, F)[:, L - (width - 1):]
    return act, tail


def _ffn_act_kernel(g_ref, u_ref, prev_ref, cw_ref, cb_ref, o_ref, *, seg, carried):
    prev = prev_ref[...]
    if carried:
        prev = jnp.where(pl.program_id(2) == 0, 0.0, prev)
    gate = _causal_conv(g_ref[...], prev, cw_ref[...], cb_ref[...], seg, carried)
    o_ref[...] = (jax.nn.gelu(gate, approximate=True) * u_ref[...]).astype(o_ref.dtype)


def ffn_act(up, nseq, L, state_conv, conv_w, conv_b, *, rows=256, cb=1536):
    M = up.shape[0]
    F = conv_w.shape[1]
    cb = _tile(F, cb, LANES)
    ncb = F // cb
    if state_conv is None:
        R = _tile(L, rows)
        nch = L // R
        in_specs = [pl.BlockSpec((R, cb), lambda b, j, c: (b * nch + c, j)),
                    pl.BlockSpec((R, cb), lambda b, j, c: (b * nch + c, ncb + j)),
                    pl.BlockSpec((SUBLANES, cb),
                                 lambda b, j, c: (jnp.maximum((b * nch + c) * (R // SUBLANES) - 1, 0), j)),
                    pl.BlockSpec((conv_w.shape[0], cb), lambda b, j, c: (0, j)),
                    pl.BlockSpec((1, cb), lambda b, j, c: (0, j))]
        return pl.pallas_call(
            functools.partial(_ffn_act_kernel, seg=R, carried=True),
            grid=(nseq, ncb, nch),
            in_specs=in_specs,
            out_specs=pl.BlockSpec((R, cb), lambda b, j, c: (b * nch + c, j)),
            out_shape=jax.ShapeDtypeStruct((M, F), BF16),
            compiler_params=_cparams("parallel", "parallel", "arbitrary"),
            name="ffn_act_prompt",
        )(up, up, up, conv_w, conv_b)
    R = _tile(M, rows)
    prev = _pad_state_rows(state_conv)
    in_specs = [pl.BlockSpec((R, cb), lambda i, j: (i, j)),
                pl.BlockSpec((R, cb), lambda i, j: (i, ncb + j)),
                pl.BlockSpec((R, cb), lambda i, j: (i, j)),
                pl.BlockSpec((conv_w.shape[0], cb), lambda i, j: (0, j)),
                pl.BlockSpec((1, cb), lambda i, j: (0, j))]
    return pl.pallas_call(
        functools.partial(_ffn_act_kernel, seg=L, carried=False),
        grid=(M // R, ncb),
        in_specs=in_specs,
        out_specs=pl.BlockSpec((R, cb), lambda i, j: (i, j)),
        out_shape=jax.ShapeDtypeStruct((M, F), BF16),
        compiler_params=_cparams("parallel", "parallel"),
        name="ffn_act_sample",
    )(up, up, prev, conv_w, conv_b)


def _norm_kernel(x_ref, g_ref, o_ref):
    o_ref[...] = _rms(x_ref[...], g_ref[...])


def rms_norm_rows(x, g, *, tm=512):
    M, D = x.shape
    tm = _tile(M, tm)
    return pl.pallas_call(
        _norm_kernel,
        grid=(M // tm,),
        in_specs=[pl.BlockSpec((tm, D), lambda i: (i, 0)), pl.BlockSpec((1, D), lambda i: (0, 0))],
        out_specs=pl.BlockSpec((tm, D), lambda i: (i, 0)),
        out_shape=jax.ShapeDtypeStruct((M, D), F32),
        compiler_params=_cparams("parallel"),
        name="final_norm",
    )(x, g.reshape(1, D))


def _pad_cols(w, width):
    return jnp.pad(w, ((0, 0), (0, width - w.shape[1])))


def _rotate_half_cols(w):
    half = w.shape[-1] // 2
    return jnp.concatenate([-w[..., half:], w[..., :half]], axis=-1)


def _row(v, width=None):
    v = v.reshape(1, -1).astype(F32)
    return v if width is None else _pad_cols(v, width)


def _layer_weights(l, dims, w_in, p):
    D, W, qr, R, rope, inner, cdim, heads_s = (dims[k] for k in
                                                ("D", "lru", "q_rank", "kv_rank", "rope", "inner", "cdim", "ssd_heads"))
    splits = (W, qr, R, rope, inner, cdim, heads_s, 3 * D)
    offs = [0]
    for s in splits:
        offs.append(offs[-1] + s)
    w = w_in[l]
    w_lru, w_qc, w_ckv, w_kr, w_z, w_xbc, w_dt, w_gate = (w[:, offs[i]:offs[i + 1]] for i in range(8))
    small = jnp.concatenate([_pad_cols(w_kr, LANES), _pad_cols(_rotate_half_cols(w_kr), LANES),
                             _pad_cols(w_dt, 2 * LANES)], axis=1)
    assert small.shape[1] == PROJ_BLOCK
    w_main = jnp.concatenate([w_lru, w_z, w_xbc, w_gate, w_qc, w_ckv, small], axis=1).astype(BF16)
    cols, o = {}, 0
    for name, width in (("lru", W), ("z", inner), ("xbc", cdim), ("gate", 3 * D), ("qc", qr), ("ckv", R),
                        ("small", PROJ_BLOCK)):
        assert o % PROJ_BLOCK == 0
        cols[name] = o
        o += width

    H, nope = dims["heads"], dims["nope"]
    hw = nope + LANES
    wq = p["mla_w_qb"][l].reshape(qr, H, nope + rope)
    wq_n, wq_r = wq[..., :nope], wq[..., nope:]
    zpad = jnp.zeros((qr, H, LANES - rope), F32)
    wq_a = jnp.concatenate([wq_n, wq_r, zpad], axis=-1).reshape(qr, H * hw)
    wq_b = jnp.concatenate([_rotate_half_cols(wq_r), zpad], axis=-1).reshape(qr, H * LANES)
    w_uk = p["mla_w_uk"][l]
    w_uv = p["mla_w_uv"][l]
    wts = {
        "w_main": w_main,
        "norm_mix": p["norm_mix"][l],
        "lru_conv_w": p["lru_conv_w"][l], "lru_conv_b": _row(p["lru_conv_b"][l]),
        "lru_w_a": p["lru_w_a"][l].astype(BF16), "lru_w_x": p["lru_w_x"][l].astype(BF16),
        "lru_b_a": _row(p["lru_b_a"][l]), "lru_b_x": _row(p["lru_b_x"][l]), "lru_lambda": _row(p["lru_lambda"][l]),
        "mla_q_norm": p["mla_q_norm"][l], "mla_kv_norm": p["mla_kv_norm"][l],
        "wq": jnp.concatenate([wq_a, wq_b], axis=1).astype(BF16),
        "w_uk_flat": w_uk.reshape(R, H * nope).astype(BF16),
        "w_uv_flat": w_uv.reshape(R, -1).astype(BF16),
        "w_uk_heads": jnp.transpose(w_uk, (1, 2, 0)).astype(BF16),
        "w_uv_heads": jnp.transpose(w_uv, (1, 0, 2)).astype(BF16),
        "ssd_heads": heads_s,
        "ssd_conv_w": p["ssd_conv_w"][l], "ssd_conv_b": _row(p["ssd_conv_b"][l]),
        "ssd_dt_bias": _row(p["ssd_dt_bias"][l], LANES), "ssd_a_log": _row(p["ssd_a_log"][l], LANES),
        "ssd_d_x": _row(jnp.repeat(p["ssd_d"][l], SSD_HEAD_DIM)), "ssd_norm": _row(p["ssd_norm"][l]),
        "layer": l,
        "w_out": p["w_out"][l].astype(BF16),
        "norm_xa": p["norm_xa"][l], "norm_mem": p["norm_mem"][l],
        "xa_wq": p["xa_wq"].astype(BF16),
        "xa_wkv": jnp.concatenate([p["xa_wk"][l], p["xa_wv"][l]], axis=1).astype(BF16),
        "xa_wo": p["xa_wo"].astype(BF16),
        "norm_ffn": p["norm_ffn"][l],
        "ffn_w_up": p["ffn_w_up"].astype(BF16), "ffn_w_down": p["ffn_w_down"].astype(BF16),
        "ffn_conv_w": p["ffn_conv_w"][l], "ffn_conv_b": _row(p["ffn_conv_b"][l]),
    }
    return wts, cols


def _rope_tables(pos0, L, rope, nope, rows, qscale):
    inv = 1.0 / (ROPE_THETA ** (jnp.arange(0, rope, 2, dtype=F32) / rope))
    ang = (pos0 + jnp.arange(L)).astype(F32)[:, None] * inv[None, :]
    cos, sin = jnp.cos(ang), jnp.sin(ang)
    cos2, sin2 = jnp.concatenate([cos, cos], axis=1), jnp.concatenate([sin, sin], axis=1)
    zpad = jnp.zeros((L, LANES - rope), F32)
    ck, sk = jnp.concatenate([cos2, zpad], axis=1), jnp.concatenate([sin2, zpad], axis=1)
    cq = jnp.concatenate([jnp.ones((L, nope), F32), ck], axis=1) * qscale
    sq = jnp.concatenate([jnp.zeros((L, nope), F32), sk], axis=1) * qscale
    reps = max(rows // L, 1)
    return tuple(jnp.tile(t, (reps, 1)) for t in (cq, sq, ck, sk))


def _trunk_layer(x, nseq, L, wts, cols, dims, tables, state, mem, paged):
    D, W, R, rope, inner, cdim, F = (dims[k] for k in ("D", "lru", "kv_rank", "rope", "inner", "cdim", "d_ff"))
    H, nope = dims["heads"], dims["nope"]
    hw = nope + LANES
    prompt = state is None
    proj = mm(x, wts["w_main"], gain=wts["norm_mix"], name="in_proj")

    y_a = lru_mixer(proj, nseq, L, None if prompt else state["lru_h"], None if prompt else state["lru_conv"], wts)

    q2 = mm(proj, wts["wq"], gain=wts["mla_q_norm"], a_col=cols["qc"] // dims["q_rank"], tn=1024, name="q_proj")
    q_all, ckv, kr = mla_prep(q2, proj, tables, wts["mla_kv_norm"], cols)
    if prompt:
        kn = mm(ckv, wts["w_uk_flat"], out_dtype=BF16, name="k_up")
        v = mm(ckv, wts["w_uv_flat"], out_dtype=BF16, name="v_up")
        y_b = mha_prompt(q_all, kn, kr, v, nseq, L, H)
    else:
        q_lat = hmm(q_all, wts["w_uk_heads"], a_stride=hw // nope, out_dtype=BF16, name="q_absorb")
        o_lat = mqa_paged(q_lat, q_all, ckv, kr, paged["pool_c"], paged["pool_kt"], paged["layer"],
                          paged["page_table"], H)
        y_b = hmm(o_lat, wts["w_uv_heads"], name="v_up_heads")

    if prompt:
        y_c, ssd_h = ssd_mixer(proj, nseq, L, None, None, None, None, wts, cols)
    else:
        y_c, ssd_h = ssd_mixer(proj, nseq, L, state["ssd_h"], state["layer"], state["ssd_h_out"],
                               state["ssd_conv"], wts, cols)

    x = merge_out(proj, y_a, y_b, y_c, wts["w_out"], x, cols)

    q = mm(x, wts["xa_wq"], gain=wts["norm_xa"], layer=wts["layer"], name="xa_q")
    if prompt:
        o = cross_attend(q, mem, mem, 0, 1, nseq, L, dims["mem_len"], dims["xa_heads"])
    else:
        o = cross_attend_cached(q, mem[0], mem[1], mem[2], L)
    x = mm(o, wts["xa_wo"], res=x, layer=wts["layer"], name="xa_o")

    act, ffn_tail = ffn_up_act(x, wts["norm_ffn"], wts["ffn_w_up"], wts["layer"], nseq, L,
                               None if prompt else state["ffn_conv"], wts["ffn_conv_w"], wts["ffn_conv_b"])
    x = mm(act, wts["ffn_w_down"], res=x, layer=wts["layer"], name="ffn_down")

    def tail(a, c0, width, n):
        return a.reshape(nseq, L, -1)[:, L - n:, c0:c0 + width]

    rows = (ckv.reshape(nseq, L, R), kr.reshape(nseq, L, LANES)[..., :rope],
            y_a.reshape(nseq, L, W)[:, L - 1], tail(proj, cols["lru"], W, 3),
            ssd_h, tail(proj, cols["xbc"], cdim, 3), ffn_tail)
    return x, rows


def kernel(x_prompt, x_sample, cache_ckv, cache_krope, cache_mem_k, cache_mem_v, state_lru_h, state_lru_conv, state_ssd_h, state_ssd_conv, state_ffn_conv, page_table, mem_prompt, norm_mix, w_in, lru_conv_w, lru_conv_b, lru_w_a, lru_b_a, lru_w_x, lru_b_x, lru_lambda, mla_q_norm, mla_w_qb, mla_kv_norm, mla_w_uk, mla_w_uv, ssd_conv_w, ssd_conv_b, ssd_dt_bias, ssd_a_log, ssd_d, ssd_norm, w_out, norm_xa, norm_mem, xa_wq, xa_wk, xa_wv, xa_wo, norm_ffn, ffn_w_up, ffn_conv_w, ffn_conv_b, ffn_w_down, norm_final):
    p = dict(norm_mix=norm_mix, lru_conv_w=lru_conv_w, lru_conv_b=lru_conv_b, lru_w_a=lru_w_a, lru_b_a=lru_b_a,
             lru_w_x=lru_w_x, lru_b_x=lru_b_x, lru_lambda=lru_lambda, mla_q_norm=mla_q_norm, mla_w_qb=mla_w_qb,
             mla_kv_norm=mla_kv_norm, mla_w_uk=mla_w_uk, mla_w_uv=mla_w_uv, ssd_conv_w=ssd_conv_w,
             ssd_conv_b=ssd_conv_b, ssd_dt_bias=ssd_dt_bias, ssd_a_log=ssd_a_log, ssd_d=ssd_d, ssd_norm=ssd_norm,
             w_out=w_out, norm_xa=norm_xa, norm_mem=norm_mem, xa_wq=xa_wq, xa_wk=xa_wk, xa_wv=xa_wv, xa_wo=xa_wo,
             norm_ffn=norm_ffn, ffn_w_up=ffn_w_up, ffn_conv_w=ffn_conv_w, ffn_conv_b=ffn_conv_b,
             ffn_w_down=ffn_w_down)
    depth = w_in.shape[0]
    bp, lp, D = x_prompt.shape
    bs, ls, _ = x_sample.shape
    n_pages = page_table.shape[1]
    page = cache_ckv.shape[2]
    mem_len, xa_heads, xa_hd = cache_mem_k.shape[2:]
    R, H, nope = mla_w_uk.shape[1:]
    dims = dict(D=D, lru=lru_conv_w.shape[-1], q_rank=mla_q_norm.shape[-1], kv_rank=R, rope=cache_krope.shape[-1],
                inner=ssd_norm.shape[-1], cdim=ssd_conv_w.shape[-1], ssd_heads=ssd_a_log.shape[-1],
                d_ff=ffn_conv_w.shape[-1], heads=H, nope=nope, mem_len=mem_len, xa_heads=xa_heads)
    xa_w = xa_heads * xa_hd
    qscale = float(nope + dims["rope"]) ** -0.5 * LOG2_E
    tab_p = _rope_tables(0, lp, dims["rope"], nope, lp, qscale)
    tab_s = _rope_tables(n_pages * page, ls, dims["rope"], nope, min(256, bs * ls), qscale)
    pool_kt = jnp.swapaxes(cache_krope, 2, 3)

    xp = x_prompt.reshape(bp * lp, D)
    xs = x_sample.reshape(bs * ls, D)
    mem_rows = mem_prompt.reshape(bp * mem_len, D)
    p_rows, s_rows, p_mem_k, p_mem_v = [], [], [], []
    s_ssd_h = None
    for l in range(depth):
        wts, cols = _layer_weights(l, dims, w_in, p)
        mkv = mm(mem_rows, wts["xa_wkv"], gain=wts["norm_mem"], name="mem_kv")
        p_mem_k.append(mkv[:, :xa_w].reshape(bp, mem_len, xa_heads, xa_hd))
        p_mem_v.append(mkv[:, xa_w:].reshape(bp, mem_len, xa_heads, xa_hd))
        xp, rows = _trunk_layer(xp, bp, lp, wts, cols, dims, tab_p, None, mkv, None)
        p_rows.append(rows)
        state = dict(lru_h=state_lru_h[l], lru_conv=state_lru_conv[l], ssd_h=state_ssd_h, layer=l,
                     ssd_h_out=s_ssd_h, ssd_conv=state_ssd_conv[l], ffn_conv=state_ffn_conv[l])
        paged = dict(pool_c=cache_ckv, pool_kt=pool_kt, layer=l, page_table=page_table)
        xs, rows = _trunk_layer(xs, bs, ls, wts, cols, dims, tab_s, state, (cache_mem_k, cache_mem_v, l), paged)
        s_ssd_h = rows[4]
        s_rows.append(rows)
    y_prompt = rms_norm_rows(xp, norm_final).reshape(bp, lp, D)
    y_sample = rms_norm_rows(xs, norm_final).reshape(bs, ls, D)
    p_out = [jnp.stack(r) for r in zip(*p_rows)]
    s_out = [s_ssd_h if i == 4 else jnp.stack(r) for i, r in enumerate(zip(*s_rows))]
    return (y_prompt, y_sample, *p_out, jnp.stack(p_mem_k), jnp.stack(p_mem_v), *s_out)
```
